```python
import jax, jax.numpy as jnp
from jax import lax
import numpy as np

D_MODEL = 1024
BATCH = 8
SEQ = 4096
DEPTH = 1

CTX_LEN = 256
GRID_W = 64
RWKV_HEADS = 8
RWKV_HEAD_DIM = 64
RWKV_WIDTH = RWKV_HEADS * RWKV_HEAD_DIM
RWKV_W_RANK = 64
RWKV_A_RANK = 64
RWKV_G_RANK = 128
LNX_EPS = 64e-5
LRU_BLOCKS = 8
LRU_BLOCK_DIM = 64
LRU_WIDTH = LRU_BLOCKS * LRU_BLOCK_DIM
CONV_W = 4
LRU_C = 8.0
N_EXPERTS = 16
EC_CAPACITY = 2
D_EXPERT = 1024
N_DIR = 2
RMS_EPS = 1e-6
RWKV_COLS = 3 * RWKV_WIDTH + RWKV_W_RANK + RWKV_A_RANK + RWKV_G_RANK
LRU_COLS = 2 * LRU_WIDTH
GATE_COLS = 2 * D_MODEL
IN_COLS = RWKV_COLS + LRU_COLS + GATE_COLS

kernel_name = 'hybrid_rwkv7_rglru_ec_moe_dit'


def rms_norm(x, g):
    xf = x.astype(jnp.float32)
    y = xf * lax.rsqrt(jnp.mean(xf * xf, axis=-1, keepdims=True) + RMS_EPS)
    return (y * g.astype(jnp.float32)).astype(x.dtype)


def modulate(h, shift, scale):
    return h * (1.0 + scale) + shift


def in_dir(z, d):
    return z if d == 0 else jnp.flip(z, axis=1)


def to_col_major(z):
    b, l, ch = z.shape
    rows = l // GRID_W
    return z.reshape(b, rows, GRID_W, ch).transpose(0, 2, 1, 3).reshape(b, l, ch)


def to_raster(z):
    b, l, ch = z.shape
    rows = l // GRID_W
    return z.reshape(b, GRID_W, rows, ch).transpose(0, 2, 1, 3).reshape(b, l, ch)


def centred_shift_mix(p, mu):
    pad = jnp.pad(p, ((0, 0), (1, 1), (0, 0)))
    nb = 0.5 * (pad[:, :-2] + pad[:, 2:])
    return p + mu * (nb - p)


def rwkv7_shared(p, lp):
    b, t, _ = p.shape
    splits = [RWKV_WIDTH, 2 * RWKV_WIDTH, 3 * RWKV_WIDTH,
              3 * RWKV_WIDTH + RWKV_W_RANK, 3 * RWKV_WIDTH + RWKV_W_RANK + RWKV_A_RANK]
    r, k, v, wd, ad, gd = jnp.split(p, splits, axis=-1)
    heads = lambda z: z.reshape(b, t, RWKV_HEADS, RWKV_HEAD_DIM)
    kk = heads((k * lp['k_k']).astype(jnp.float32))
    kk = kk / jnp.maximum(jnp.sqrt(jnp.sum(kk * kk, axis=-1, keepdims=True)), 1e-12)
    g = jax.nn.sigmoid(gd) @ lp['g2']
    return {'r': heads(r), 'k': k, 'v': heads(v), 'kk': kk,
            'wd': jnp.tanh(wd), 'ad': ad, 'g': g}


def rwkv7_direction_inputs(f, lp, d):
    b, t, _ = f['k'].shape
    heads = lambda z: z.reshape(b, t, RWKV_HEADS, RWKV_HEAD_DIM)
    w = -jax.nn.softplus(-(lp['w0'][d] + f['wd'] @ lp['w2'][d])) - 0.5
    decay = jnp.exp(-jnp.exp(w.astype(jnp.float32)))
    a = jax.nn.sigmoid(lp['a0'][d] + f['ad'] @ lp['a2'][d])
    k = f['k'] * (1.0 + (a - 1.0) * lp['k_a'])
    return (f['r'], heads(decay), heads(k), f['v'], f['kk'], heads(a))


def rwkv7_scan(inputs, s0):
    def step(s, inp):
        r_t, w_t, k_t, v_t, kk_t, a_t = inp
        sa = jnp.einsum('bhvk,bhk->bhv', s, -kk_t)
        s = (s * w_t[:, :, None, :] + sa[..., None] * (kk_t * a_t)[:, :, None, :]
             + v_t[..., None] * k_t[:, :, None, :])
        return s, jnp.einsum('bhvk,bhk->bhv', s, r_t)
    xs = tuple(jnp.moveaxis(z.astype(jnp.float32), 1, 0) for z in inputs)
    s_last, y = lax.scan(step, s0, xs)
    return jnp.moveaxis(y, 0, 1), s_last


def rwkv7_output(y, f, k_sum, lp):
    b, t = y.shape[:2]
    mean = jnp.mean(y, axis=-1, keepdims=True)
    var = jnp.mean(jnp.square(y - mean), axis=-1, keepdims=True)
    yn = ((y - mean) * lax.rsqrt(var + LNX_EPS)).reshape(b, t, RWKV_WIDTH) * lp['lnx_g'] + lp['lnx_b']
    bonus = jnp.sum(f['r'] * k_sum * lp['r_k'], axis=-1, keepdims=True) * f['v']
    o = (yn + bonus.reshape(b, t, RWKV_WIDTH)) * f['g']
    return o @ lp['w_proj_rwkv']


def rwkv7_mixer(p_ctx, p_lat, lp, need_ctx):
    f_ctx = rwkv7_shared(p_ctx, lp)
    f_lat = rwkv7_shared(p_lat, lp)
    b = p_lat.shape[0]
    y_lat, k_lat, y_ctx, k_ctx = [], [], [], []
    for d in range(N_DIR):
        in_c = rwkv7_direction_inputs(f_ctx, lp, d)
        in_l = rwkv7_direction_inputs(f_lat, lp, d)
        s0 = jnp.zeros((b, RWKV_HEADS, RWKV_HEAD_DIM, RWKV_HEAD_DIM), jnp.float32)
        yc, s_ctx = rwkv7_scan(tuple(in_dir(z, d) for z in in_c), s0)
        yl, _ = rwkv7_scan(tuple(in_dir(z, d) for z in in_l), s_ctx)
        y_lat.append(in_dir(yl, d))
        k_lat.append(in_l[2])
        if need_ctx:
            y_ctx.append(in_dir(yc, d))
            k_ctx.append(in_c[2])
    out_lat = rwkv7_output(y_lat[0] + y_lat[1], f_lat, k_lat[0] + k_lat[1], lp)
    out_ctx = rwkv7_output(y_ctx[0] + y_ctx[1], f_ctx, k_ctx[0] + k_ctx[1], lp) if need_ctx else None
    return out_lat, out_ctx


def causal_dwconv(z, w, bias):
    t = z.shape[1]
    zp = jnp.pad(z, ((0, 0), (CONV_W - 1, 0), (0, 0)))
    y = bias + w[0] * zp[:, 0:t]
    for j in range(1, CONV_W):
        y = y + w[j] * zp[:, j:j + t]
    return y


def block_diag_linear(z, w, bias):
    b, t, _ = z.shape
    y = jnp.einsum('bthi,hij->bthj', z.reshape(b, t, LRU_BLOCKS, LRU_BLOCK_DIM), w)
    return y.reshape(b, t, LRU_WIDTH) + bias


def lru_combine(left, right):
    a1, b1 = left
    a2, b2 = right
    return a1 * a2, a2 * b1 + b2


def rglru_direction(xb, lp, d, h0):
    xc = causal_dwconv(xb.astype(jnp.float32), lp['conv_w'][d], lp['conv_b'][d])
    rg = jax.nn.sigmoid(block_diag_linear(xc, lp['wa'][d], lp['ba'][d]))
    ig = jax.nn.sigmoid(block_diag_linear(xc, lp['wx'][d], lp['bx'][d]))
    log_a = -LRU_C * rg * jax.nn.softplus(-lp['lam'][d])
    a = jnp.exp(log_a)
    bt = jnp.sqrt(-jnp.expm1(2.0 * log_a)) * (ig * xc)
    bt = bt.at[:, 0].add(a[:, 0] * h0)
    _, h = lax.associative_scan(lru_combine, (a, bt), axis=1)
    return h, h[:, -1]


def rglru_mixer(xb_ctx, gb_ctx, xb_lat, gb_lat, lp, need_ctx):
    b = xb_lat.shape[0]
    xl = to_col_major(xb_lat)
    h_lat, h_ctx = [], []
    for d in range(N_DIR):
        h0 = jnp.zeros((b, LRU_WIDTH), jnp.float32)
        hc, hc_last = rglru_direction(in_dir(xb_ctx, d), lp, d, h0)
        hl, _ = rglru_direction(in_dir(xl, d), lp, d, hc_last)
        h_lat.append(in_dir(hl, d))
        if need_ctx:
            h_ctx.append(in_dir(hc, d))
    out_lat = (to_raster(h_lat[0] + h_lat[1]) * jax.nn.gelu(gb_lat)) @ lp['w_proj_lru']
    out_ctx = ((h_ctx[0] + h_ctx[1]) * jax.nn.gelu(gb_ctx)) @ lp['w_proj_lru'] if need_ctx else None
    return out_lat, out_ctx


def split_columns(p):
    o1 = RWKV_COLS
    o2 = o1 + LRU_WIDTH
    o3 = o1 + LRU_COLS
    return p[..., :o1], p[..., o1:o2], p[..., o2:o3], p[..., o3:]


def branch_merge(gates, y_a, y_b, w_out):
    g_a, g_b = jnp.split(gates, 2, axis=-1)
    return (jax.nn.sigmoid(g_a) * y_a + jax.nn.sigmoid(g_b) * y_b) @ w_out


def mixer_sublayer(u_lat, u_ctx, lp, need_ctx):
    rw_l, xb_l, gb_l, gt_l = split_columns(u_lat @ lp['w_in'])
    rw_c, xb_c, gb_c, gt_c = split_columns(u_ctx @ lp['w_in'])
    ya_l, ya_c = rwkv7_mixer(centred_shift_mix(rw_c, lp['mu']), centred_shift_mix(rw_l, lp['mu']), lp, need_ctx)
    yb_l, yb_c = rglru_mixer(xb_c, gb_c, xb_l, gb_l, lp, need_ctx)
    out_lat = branch_merge(gt_l, ya_l, yb_l, lp['w_out'])
    out_ctx = branch_merge(gt_c, ya_c, yb_c, lp['w_out']) if need_ctx else None
    return out_lat, out_ctx


def expert_choice_moe(h, lp):
    b, t, dm = h.shape
    cap = EC_CAPACITY * t // N_EXPERTS
    affinity = jax.nn.softmax((h @ lp['router']).astype(jnp.float32), axis=-1)
    gate, idx = lax.top_k(jnp.swapaxes(affinity, 1, 2), cap)
    xs = jax.vmap(lambda hb, ib: hb[ib])(h, idx)
    hid = (jax.nn.silu(jnp.einsum('becd,edf->becf', xs, lp['moe_w1']))
           * jnp.einsum('becd,edf->becf', xs, lp['moe_w3']))
    y = jnp.einsum('becf,efd->becd', hid, lp['moe_w2']) * gate[..., None].astype(h.dtype)
    return jax.vmap(lambda yb, ib: jnp.zeros((t, dm), y.dtype).at[ib.reshape(-1)].add(yb.reshape(-1, dm)))(y, idx)


def setup_inputs(seed: int = 0) -> dict:
    key = jax.random.key(seed)
    ks = iter(jax.random.split(key, 48))
    f32 = jnp.float32
    L = DEPTH
    D = D_MODEL

    def nrm(shape, scale):
        return jax.random.normal(next(ks), shape, f32) * scale

    x = nrm((BATCH, SEQ, D), 1.0)
    c = nrm((BATCH, D), 1.0)
    ctx = nrm((BATCH, CTX_LEN, D), 1.0)
    c_ctx = nrm((D,), 1.0)
    w_mod = nrm((L, D, 6 * D), 0.3 * D ** -0.5)
    b_mod = nrm((L, 6 * D), 0.02)
    norm1_g = 1.0 + nrm((L, D), 0.02)
    norm2_g = 1.0 + nrm((L, D), 0.02)
    w_in = nrm((L, D, IN_COLS), D ** -0.5)
    mu_rwkv = jax.random.uniform(next(ks), (L, RWKV_COLS), f32, 0.0, 1.0)
    rwkv_w0 = jax.random.uniform(next(ks), (L, N_DIR, RWKV_WIDTH), f32, -6.0, -1.0)
    rwkv_w2 = nrm((L, N_DIR, RWKV_W_RANK, RWKV_WIDTH), 0.1 * RWKV_W_RANK ** -0.5)
    rwkv_a0 = nrm((L, N_DIR, RWKV_WIDTH), 0.1)
    rwkv_a2 = nrm((L, N_DIR, RWKV_A_RANK, RWKV_WIDTH), 0.5 * RWKV_A_RANK ** -0.5)
    rwkv_g2 = nrm((L, RWKV_G_RANK, RWKV_WIDTH), RWKV_G_RANK ** -0.5)
    rwkv_k_k = 0.85 + nrm((L, RWKV_WIDTH), 0.02)
    rwkv_k_a = 1.0 + nrm((L, RWKV_WIDTH), 0.02)
    rwkv_r_k = nrm((L, RWKV_HEADS, RWKV_HEAD_DIM), 0.1)
    lnx_g = 1.0 + nrm((L, RWKV_WIDTH), 0.02)
    lnx_b = nrm((L, RWKV_WIDTH), 0.02)
    w_proj_rwkv = nrm((L, RWKV_WIDTH, D), RWKV_WIDTH ** -0.5)
    conv_w = nrm((L, N_DIR, CONV_W, LRU_WIDTH), CONV_W ** -0.5)
    conv_b = nrm((L, N_DIR, LRU_WIDTH), 0.02)
    lru_wa = nrm((L, N_DIR, LRU_BLOCKS, LRU_BLOCK_DIM, LRU_BLOCK_DIM), LRU_BLOCK_DIM ** -0.5)
    lru_ba = nrm((L, N_DIR, LRU_WIDTH), 0.02)
    lru_wx = nrm((L, N_DIR, LRU_BLOCKS, LRU_BLOCK_DIM, LRU_BLOCK_DIM), LRU_BLOCK_DIM ** -0.5)
    lru_bx = nrm((L, N_DIR, LRU_WIDTH), 0.02)
    a_target = jax.random.uniform(next(ks), (L, N_DIR, LRU_WIDTH), f32, 0.9, 0.999)
    p_lam = a_target ** (1.0 / LRU_C)
    lru_lam = jnp.log(p_lam) - jnp.log1p(-p_lam)
    w_proj_lru = nrm((L, LRU_WIDTH, D), LRU_WIDTH ** -0.5)
    w_out = nrm((L, D, D), D ** -0.5)
    router = nrm((L, D, N_EXPERTS), D ** -0.5)
    moe_w1 = nrm((L, N_EXPERTS, D, D_EXPERT), D ** -0.5)
    moe_w3 = nrm((L, N_EXPERTS, D, D_EXPERT), D ** -0.5)
    moe_w2 = nrm((L, N_EXPERTS, D_EXPERT, D), D_EXPERT ** -0.5)
    final_g = 1.0 + nrm((D,), 0.02)
    return {'x': x, 'c': c, 'ctx': ctx, 'c_ctx': c_ctx, 'w_mod': w_mod, 'b_mod': b_mod,
            'norm1_g': norm1_g, 'norm2_g': norm2_g, 'w_in': w_in, 'mu_rwkv': mu_rwkv,
            'rwkv_w0': rwkv_w0, 'rwkv_w2': rwkv_w2, 'rwkv_a0': rwkv_a0, 'rwkv_a2': rwkv_a2,
            'rwkv_g2': rwkv_g2, 'rwkv_k_k': rwkv_k_k, 'rwkv_k_a': rwkv_k_a, 'rwkv_r_k': rwkv_r_k,
            'lnx_g': lnx_g, 'lnx_b': lnx_b, 'w_proj_rwkv': w_proj_rwkv, 'conv_w': conv_w,
            'conv_b': conv_b, 'lru_wa': lru_wa, 'lru_ba': lru_ba, 'lru_wx': lru_wx, 'lru_bx': lru_bx,
            'lru_lam': lru_lam, 'w_proj_lru': w_proj_lru, 'w_out': w_out, 'router': router,
            'moe_w1': moe_w1, 'moe_w3': moe_w3, 'moe_w2': moe_w2, 'final_g': final_g}


def reference(x, c, ctx, c_ctx, w_mod, b_mod, norm1_g, norm2_g, w_in, mu_rwkv,
              rwkv_w0, rwkv_w2, rwkv_a0, rwkv_a2, rwkv_g2, rwkv_k_k, rwkv_k_a, rwkv_r_k,
              lnx_g, lnx_b, w_proj_rwkv, conv_w, conv_b, lru_wa, lru_ba, lru_wx, lru_bx,
              lru_lam, w_proj_lru, w_out, router, moe_w1, moe_w3, moe_w2, final_g):
    silu_c = jax.nn.silu(c)
    silu_cc = jax.nn.silu(c_ctx)
    for layer in range(DEPTH):
        need_ctx = layer + 1 < DEPTH
        m_lat = (silu_c @ w_mod[layer] + b_mod[layer])[:, None, :]
        m_ctx = (silu_cc @ w_mod[layer] + b_mod[layer])[None, None, :]
        sh1, sc1, gt1, sh2, sc2, gt2 = jnp.split(m_lat, 6, axis=-1)
        csh1, csc1, cgt1, csh2, csc2, cgt2 = jnp.split(m_ctx, 6, axis=-1)
        lp = {'w_in': w_in[layer], 'mu': mu_rwkv[layer], 'w0': rwkv_w0[layer], 'w2': rwkv_w2[layer],
              'a0': rwkv_a0[layer], 'a2': rwkv_a2[layer], 'g2': rwkv_g2[layer], 'k_k': rwkv_k_k[layer],
              'k_a': rwkv_k_a[layer], 'r_k': rwkv_r_k[layer], 'lnx_g': lnx_g[layer], 'lnx_b': lnx_b[layer],
              'w_proj_rwkv': w_proj_rwkv[layer], 'conv_w': conv_w[layer], 'conv_b': conv_b[layer],
              'wa': lru_wa[layer], 'ba': lru_ba[layer], 'wx': lru_wx[layer], 'bx': lru_bx[layer],
              'lam': lru_lam[layer], 'w_proj_lru': w_proj_lru[layer], 'w_out': w_out[layer],
              'router': router[layer], 'moe_w1': moe_w1[layer], 'moe_w3': moe_w3[layer],
              'moe_w2': moe_w2[layer]}
        u_lat = modulate(rms_norm(x, norm1_g[layer]), sh1, sc1)
        u_ctx = modulate(rms_norm(ctx, norm1_g[layer]), csh1, csc1)
        mix_lat, mix_ctx = mixer_sublayer(u_lat, u_ctx, lp, need_ctx)
        x = x + gt1 * mix_lat
        x = x + gt2 * expert_choice_moe(modulate(rms_norm(x, norm2_g[layer]), sh2, sc2), lp)
        if need_ctx:
            ctx = ctx + cgt1 * mix_ctx
            ctx = ctx + cgt2 * expert_choice_moe(modulate(rms_norm(ctx, norm2_g[layer]), csh2, csc2), lp)
    return rms_norm(x, final_g)
```

```python
import functools

import jax
import jax.numpy as jnp
from jax import lax
from jax.experimental import pallas as pl
from jax.experimental.pallas import tpu as pltpu

F32 = jnp.float32
BF16 = jnp.bfloat16

GRID_W = 64
HEAD_DIM = 64
RWKV_W_RANK = 64
RWKV_A_RANK = 64
RWKV_G_RANK = 128
CONV_W = 4
LRU_C = 8.0
EC_CAPACITY = 2
RMS_EPS = 1e-6
LNX_EPS = 64e-5

LANES = 128
SUBLANES = 8
CHUNK = 64
ROW_TILE = 256
VMEM_LIMIT = 56 * 1024 * 1024


def _cparams(sem):
    return pltpu.CompilerParams(dimension_semantics=sem, vmem_limit_bytes=VMEM_LIMIT)


def _mm(a, b):
    return jnp.dot(a.astype(BF16), b.astype(BF16), preferred_element_type=F32)


def _mm_nt(a, b):
    return lax.dot_general(a.astype(BF16), b.astype(BF16), (((1,), (1,)), ((), ())),
                           preferred_element_type=F32)


def _mm_tn(a, b):
    return lax.dot_general(a.astype(BF16), b.astype(BF16), (((0,), (0,)), ((), ())),
                           preferred_element_type=F32)


def _split3(a):
    hi = a.astype(BF16)
    r1 = a - hi.astype(F32)
    mid = r1.astype(BF16)
    lo = (r1 - mid.astype(F32)).astype(BF16)
    return hi, mid, lo


def _mm_exact_rhs(a, b01):
    hi, mid, lo = _split3(a)
    d = lambda z: jnp.dot(z, b01, preferred_element_type=F32)
    return d(hi) + d(mid) + d(lo)


def _mm_exact_lhs(a01, b):
    hi, mid, lo = _split3(b)
    d = lambda z: jnp.dot(a01, z, preferred_element_type=F32)
    return d(hi) + d(mid) + d(lo)


def _mm3(a, b):
    ah = a.astype(BF16)
    al = (a - ah.astype(F32)).astype(BF16)
    bh = b.astype(BF16)
    bl = (b - bh.astype(F32)).astype(BF16)
    d = lambda x, y: jnp.dot(x, y, preferred_element_type=F32)
    return d(ah, bh) + d(al, bh) + d(ah, bl)


def _softplus(z):
    return jnp.maximum(z, 0.0) + jnp.log(1.0 + jnp.exp(-jnp.abs(z)))


def _sigmoid(z):
    return 1.0 / (1.0 + jnp.exp(-z))


def _mod_kernel(c_ref, w_ref, b_ref, o_ref):
    c = c_ref[...]
    s = c * _sigmoid(c)
    o_ref[...] = jnp.dot(s, w_ref[...], precision=lax.Precision.HIGHEST,
                         preferred_element_type=F32) + b_ref[...]


def _modulation(c_rows, w_mod, b_mod):
    rows, d = c_rows.shape
    n = w_mod.shape[1]
    tn = 1536
    return pl.pallas_call(
        _mod_kernel,
        grid=(n // tn,),
        in_specs=[pl.BlockSpec((rows, d), lambda j: (0, 0)),
                  pl.BlockSpec((d, tn), lambda j: (0, j)),
                  pl.BlockSpec((1, tn), lambda j: (0, j))],
        out_specs=pl.BlockSpec((rows, tn), lambda j: (0, j)),
        out_shape=jax.ShapeDtypeStruct((rows, n), F32),
        compiler_params=_cparams(("arbitrary",)),
        name="mod",
    )(c_rows, w_mod, b_mod.reshape(1, n))


def _inproj_kernel(x_ref, g_ref, mod_ref, w_ref, rw_ref, xb_ref, ggb_ref, sg_ref, *, cols):
    x = x_ref[0]
    xn = x * lax.rsqrt(jnp.mean(x * x, axis=-1, keepdims=True) + RMS_EPS) * g_ref[...]
    sh = mod_ref[0, 0, 0:1, :]
    sc = mod_ref[0, 0, 1:2, :]
    u = (xn * (1.0 + sc) + sh).astype(BF16)
    c_rw, c_xb, c_gb = cols
    d = lambda lo, hi: jnp.dot(u, w_ref[:, lo:hi], preferred_element_type=F32)
    rw_ref[0] = d(0, c_rw)
    xb_ref[0] = d(c_rw, c_xb)
    ggb_ref[0] = jax.nn.gelu(d(c_xb, c_gb))
    sg_ref[0] = _sigmoid(d(c_gb, w_ref.shape[1]))


def _inproj(xc, g1, mod1, w_in_bf, tc, rwkv_cols, lru_w):
    b, t, d = xc.shape
    n = w_in_bf.shape[1]
    tm = min(ROW_TILE, tc)
    ntc = tc // tm
    cols = (rwkv_cols, rwkv_cols + lru_w, rwkv_cols + 2 * lru_w)
    gate_cols = n - cols[2]
    row_spec = lambda w: pl.BlockSpec((1, tm, w), lambda bi, i: (bi, i, 0))
    return pl.pallas_call(
        functools.partial(_inproj_kernel, cols=cols),
        grid=(b, t // tm),
        in_specs=[row_spec(d),
                  pl.BlockSpec((1, d), lambda bi, i: (0, 0)),
                  pl.BlockSpec((1, 1, 2, d), lambda bi, i: (bi, jnp.where(i < ntc, 0, 1), 0, 0)),
                  pl.BlockSpec((d, n), lambda bi, i: (0, 0))],
        out_specs=[row_spec(rwkv_cols), row_spec(lru_w), row_spec(lru_w), row_spec(gate_cols)],
        out_shape=[jax.ShapeDtypeStruct((b, t, rwkv_cols), F32),
                   jax.ShapeDtypeStruct((b, t, lru_w), F32),
                   jax.ShapeDtypeStruct((b, t, lru_w), F32),
                   jax.ShapeDtypeStruct((b, t, gate_cols), F32)],
        compiler_params=_cparams(("parallel", "parallel")),
        name="inproj",
    )(xc, g1, mod1, w_in_bf)


def _shift_down(z, halo_prev, s):
    tm = z.shape[0]
    rolled = pltpu.roll(z, s, 0)
    hp = pltpu.roll(halo_prev, s, 0)
    r8 = lax.broadcasted_iota(jnp.int32, (SUBLANES, 1), 0)
    head = jnp.where(r8 < s, hp, rolled[0:SUBLANES])
    return jnp.concatenate([head, rolled[SUBLANES:tm]], axis=0)


def _shift_up(z, halo_next, s):
    tm = z.shape[0]
    rolled = pltpu.roll(z, tm - s, 0)
    hn = pltpu.roll(halo_next, SUBLANES - s, 0)
    r8 = lax.broadcasted_iota(jnp.int32, (SUBLANES, 1), 0)
    tail = jnp.where(r8 >= SUBLANES - s, hn, rolled[tm - SUBLANES:tm])
    return jnp.concatenate([rolled[0:tm - SUBLANES], tail], axis=0)


def _seg_pos(i, tm, tc, tl):
    grow = lax.broadcasted_iota(jnp.int32, (tm, 1), 0) + i * tm
    in_ctx = grow < tc
    pos = jnp.where(in_ctx, grow, grow - tc)
    seg_len = jnp.where(in_ctx, tc, tl)
    return pos, seg_len


def _halo_specs(tm, t, width):
    nb = tm // SUBLANES
    last = t // SUBLANES - 1
    prev = pl.BlockSpec((1, SUBLANES, width), lambda bi, i: (bi, jnp.maximum(i * nb - 1, 0), 0))
    nxt = pl.BlockSpec((1, SUBLANES, width), lambda bi, i: (bi, jnp.minimum((i + 1) * nb, last), 0))
    return prev, nxt


def _prep_kernel(rw_ref, prev_ref, next_ref, mu_ref, kkw_ref, ka_ref, w0_ref, a0_ref,
                 w2_ref, a2_ref, g2_ref, e_ref,
                 r_ref, v_ref, kk_ref, g_ref, ks_ref, lw_ref, kd_ref, bd_ref, *, tc, tl, width):
    i = pl.program_id(1)
    p = rw_ref[0]
    tm = p.shape[0]
    pos, seg_len = _seg_pos(i, tm, tc, tl)
    prev = jnp.where(pos >= 1, _shift_down(p, prev_ref[0], 1), 0.0)
    nxt = jnp.where(pos + 1 < seg_len, _shift_up(p, next_ref[0], 1), 0.0)
    pm = p + mu_ref[...] * (0.5 * (prev + nxt) - p)

    w = width
    r = pm[:, 0:w]
    k = pm[:, w:2 * w]
    v = pm[:, 2 * w:3 * w]
    wa = pm[:, 3 * w:3 * w + RWKV_W_RANK + RWKV_A_RANK]
    gd = pm[:, 3 * w + RWKV_W_RANK + RWKV_A_RANK:]

    kkr = k * kkw_ref[...]
    ss = _mm_exact_rhs(kkr * kkr, e_ref[...])
    kk = kkr / jnp.maximum(jnp.sqrt(ss), 1e-12)
    r_ref[0] = r
    v_ref[0] = v
    kk_ref[0] = kk
    g_ref[0] = _mm(_sigmoid(gd), g2_ref[...])

    twa = jnp.tanh(wa)
    ka = ka_ref[...]
    ksum = jnp.zeros_like(k)
    for d in range(2):
        wpre = w0_ref[d:d + 1, :] + _mm(twa, w2_ref[d])
        lw_ref[d, 0] = -jnp.exp(-_softplus(-wpre) - 0.5)
        eta = _sigmoid(a0_ref[d:d + 1, :] + _mm(wa, a2_ref[d]))
        kd = k * (1.0 + (eta - 1.0) * ka)
        kd_ref[d, 0] = kd
        bd_ref[d, 0] = kk * eta
        ksum = ksum + kd
    ks_ref[0] = ksum


def _rwkv_prep(rw, mu, k_k, k_a, w0, a0, w2p, a2p, g2, e_heads, tc, tl):
    b, t, cols = rw.shape
    w = k_k.shape[-1]
    tm = min(ROW_TILE, tc)
    prev_spec, next_spec = _halo_specs(tm, t, cols)
    full = lambda shape: pl.BlockSpec(shape, lambda bi, i: (0,) * len(shape))
    row = pl.BlockSpec((1, tm, w), lambda bi, i: (bi, i, 0))
    drow = pl.BlockSpec((2, 1, tm, w), lambda bi, i: (0, bi, i, 0))
    shp = jax.ShapeDtypeStruct((b, t, w), F32)
    dshp = jax.ShapeDtypeStruct((2, b, t, w), F32)
    return pl.pallas_call(
        functools.partial(_prep_kernel, tc=tc, tl=tl, width=w),
        grid=(b, t // tm),
        in_specs=[pl.BlockSpec((1, tm, cols), lambda bi, i: (bi, i, 0)), prev_spec, next_spec,
                  full((1, cols)), full((1, w)), full((1, w)), full((2, w)), full((2, w)),
                  full(w2p.shape), full(a2p.shape), full(g2.shape), full(e_heads.shape)],
        out_specs=[row, row, row, row, row, drow, drow, drow],
        out_shape=[shp, shp, shp, shp, shp, dshp, dshp, dshp],
        compiler_params=_cparams(("parallel", "parallel")),
        name="prep",
    )(rw, rw, rw, mu, k_k, k_a, w0, a0, w2p, a2p, g2, e_heads)


def _chunk_unit(r, v, kk, lw, kd, bd, m, nsq):
    c = r.shape[0]
    lc = _mm_exact_lhs(m["cum"], lw)
    le = lc - lw
    ltot = jnp.sum(lw, axis=0, keepdims=True)
    en = jnp.exp(-lc)
    eh = jnp.exp(ltot - lc)
    qa = -kk * jnp.exp(le)
    qr = r * jnp.exp(lc)
    kb = bd * en
    kx = kd * en

    m0, m1 = m["m0"], m["m1"]
    stack = lambda z: jnp.concatenate([z * m0, z * m1], axis=0)
    dup = lambda z: jnp.concatenate([z, z], axis=0)
    qa_bd, qr_bd, v_bd = stack(qa), stack(qr), stack(v)
    kbh_bd, kxh_bd = stack(bd * eh), stack(kd * eh)
    kb2, kx2 = dup(kb), dup(kx)

    gab = jnp.where(m["strict"], _mm_nt(qa_bd, kb2), 0.0)
    gak = jnp.where(m["strict"], _mm_nt(qa_bd, kx2), 0.0)
    grb = jnp.where(m["incl"], _mm_nt(qr_bd, kb2), 0.0)
    grk = jnp.where(m["incl"], _mm_nt(qr_bd, kx2), 0.0)

    t_inv = m["eye"] + gab
    gp = gab
    for _ in range(nsq):
        gp = _mm(gp, gp)
        t_inv = t_inv + _mm(t_inv, gp)

    ta = _mm(t_inv, qa_bd)
    uv = _mm(t_inv, _mm(gak, v_bd))
    ra = qr_bd + _mm(grb, ta)
    yv = _mm(grb, uv) + _mm(grk, v_bd)
    ac = m["eye"] * jnp.exp(ltot) + _mm_tn(kbh_bd, ta)
    sc = _mm_tn(kbh_bd, uv) + _mm_tn(kxh_bd, v_bd)
    fold = lambda z: z[0:c] + z[c:2 * c]
    return fold(ra), fold(yv), fold(ac), fold(sc)


def _chunk_masks(rev):
    c = CHUNK
    n = 2 * c
    ti = lax.broadcasted_iota(jnp.int32, (c, c), 0)
    ii = lax.broadcasted_iota(jnp.int32, (c, c), 1)
    sign = jnp.where(rev, -1, 1)
    cum = jnp.where((ii - ti) * sign <= 0, 1.0, 0.0).astype(BF16)
    row = lax.broadcasted_iota(jnp.int32, (n, n), 0)
    col = lax.broadcasted_iota(jnp.int32, (n, n), 1)
    same = (row // c) == (col // c)
    tt = row % c
    it = col % c
    before = (it - tt) * sign < 0
    lane = lax.broadcasted_iota(jnp.int32, (1, n), 1)
    return {
        "cum": cum,
        "strict": same & before,
        "incl": same & ((it - tt) * sign <= 0),
        "eye": jnp.where(row == col, 1.0, 0.0).astype(F32),
        "m0": jnp.where(lane < c, 1.0, 0.0).astype(F32),
        "m1": jnp.where(lane >= c, 1.0, 0.0).astype(F32),
    }


def _chunk_kernel(r_ref, v_ref, kk_ref, lw_ref, kd_ref, bd_ref,
                  ra_ref, yv_ref, ac_ref, sc_ref, *, nsq):
    rev = pl.program_id(0) == 1
    m = _chunk_masks(rev)
    tm, w = r_ref.shape[1], r_ref.shape[2]
    n = 2 * CHUNK
    for ci in range(tm // CHUNK):
        rows = slice(ci * CHUNK, (ci + 1) * CHUNK)
        for hp in range(w // n):
            lanes = slice(hp * n, (hp + 1) * n)
            ra, yv, ac, sc = _chunk_unit(
                r_ref[0, rows, lanes], v_ref[0, rows, lanes], kk_ref[0, rows, lanes],
                lw_ref[0, 0, rows, lanes], kd_ref[0, 0, rows, lanes], bd_ref[0, 0, rows, lanes],
                m, nsq)
            ra_ref[0, 0, rows, lanes] = ra
            yv_ref[0, 0, rows, lanes] = yv
            ac_ref[0, 0, ci, hp] = ac
            sc_ref[0, 0, ci, hp] = sc


def _rwkv_chunk(r, v, kk, lw, kd, bd, tm):
    b, t, w = r.shape
    n = 2 * CHUNK
    nc = t // CHUNK
    cpt = tm // CHUNK
    nsq = (CHUNK - 1).bit_length() - 1
    row = pl.BlockSpec((1, tm, w), lambda d, bi, i: (bi, i, 0))
    drow = pl.BlockSpec((1, 1, tm, w), lambda d, bi, i: (d, bi, i, 0))
    mat = pl.BlockSpec((1, 1, cpt, w // n, CHUNK, n), lambda d, bi, i: (d, bi, i, 0, 0, 0))
    dshp = jax.ShapeDtypeStruct((2, b, t, w), F32)
    mshp = jax.ShapeDtypeStruct((2, b, nc, w // n, CHUNK, n), F32)
    return pl.pallas_call(
        functools.partial(_chunk_kernel, nsq=nsq),
        grid=(2, b, t // tm),
        in_specs=[row, row, row, drow, drow, drow],
        out_specs=[drow, drow, mat, mat],
        out_shape=[dshp, dshp, mshp, mshp],
        compiler_params=_cparams(("parallel", "parallel", "parallel")),
        name="chunk",
    )(r, v, kk, lw, kd, bd)


def _carry_kernel(ra0, yv0, ac0, sc0, ra1, yv1, ac1, sc1, y0_ref, y1_ref, st_ref):
    j = pl.program_id(0)

    @pl.when(j == 0)
    def _():
        st_ref[...] = jnp.zeros_like(st_ref)

    nb, c, w = y0_ref.shape
    n = 2 * c
    lane = lax.broadcasted_iota(jnp.int32, (1, n), 1)
    m0 = jnp.where(lane < c, 1.0, 0.0).astype(F32)
    m1 = 1.0 - m0
    stack = lambda z: jnp.concatenate([z * m0, z * m1], axis=0)

    for d, (ra, yv, ac, sc, y_ref) in enumerate(((ra0, yv0, ac0, sc0, y0_ref),
                                                 (ra1, yv1, ac1, sc1, y1_ref))):
        def body(bi, carry, d=d, ra=ra, yv=yv, ac=ac, sc=sc, y_ref=y_ref):
            for hp in range(w // n):
                lanes = slice(hp * n, (hp + 1) * n)
                st = st_ref[d, bi, hp]
                y_ref[bi, :, lanes] = _mm3(ra[0, bi, :, lanes], st) + yv[0, bi, :, lanes]
                st_ref[d, bi, hp] = _mm3(stack(ac[0, bi, 0, hp]), st) + stack(sc[0, bi, 0, hp])
            return carry
        lax.fori_loop(0, nb, body, 0)


def _rwkv_carry(ra, yv, ac, sc, tc):
    _, b, t, w = ra.shape
    n = 2 * CHUNK
    nc = t // CHUNK
    ncc = tc // CHUNK
    fwd = lambda j: j
    bwd = lambda j: jnp.where(j < ncc, ncc - 1 - j, nc + ncc - 1 - j)
    row = lambda d, f: pl.BlockSpec((1, b, CHUNK, w), lambda j: (d, 0, f(j), 0))
    mat = lambda d, f: pl.BlockSpec((1, b, 1, w // n, CHUNK, n), lambda j: (d, 0, f(j), 0, 0, 0))
    out = lambda f: pl.BlockSpec((b, CHUNK, w), lambda j: (0, f(j), 0))
    shp = jax.ShapeDtypeStruct((b, t, w), F32)
    return pl.pallas_call(
        _carry_kernel,
        grid=(nc,),
        in_specs=[row(0, fwd), row(0, fwd), mat(0, fwd), mat(0, fwd),
                  row(1, bwd), row(1, bwd), mat(1, bwd), mat(1, bwd)],
        out_specs=[out(fwd), out(bwd)],
        out_shape=[shp, shp],
        scratch_shapes=[pltpu.VMEM((2, b, w // n, n, n), F32)],
        compiler_params=_cparams(("arbitrary",)),
        name="carry",
    )(ra, yv, ac, sc, ra, yv, ac, sc)


def _gates_kernel(z_ref, prev_ref, next_ref, cw_ref, cb_ref, wa_ref, ba_ref, wx_ref, bx_ref,
                  lam_ref, a_ref, b_ref, *, tc, tl):
    i = pl.program_id(1)
    z = z_ref[0]
    tm = z.shape[0]
    pos, seg_len = _seg_pos(i, tm, tc, tl)
    for d in range(2):
        xc = cb_ref[d:d + 1, :] + cw_ref[d, CONV_W - 1:CONV_W, :] * z
        for s in range(1, CONV_W):
            if d == 0:
                zs = jnp.where(pos >= s, _shift_down(z, prev_ref[0], s), 0.0)
            else:
                zs = jnp.where(pos + s < seg_len, _shift_up(z, next_ref[0], s), 0.0)
            xc = xc + cw_ref[d, CONV_W - 1 - s:CONV_W - s, :] * zs
        rg = _sigmoid(_mm(xc, wa_ref[d]) + ba_ref[d:d + 1, :])
        ig = _sigmoid(_mm(xc, wx_ref[d]) + bx_ref[d:d + 1, :])
        log_a = -LRU_C * rg * _softplus(-lam_ref[d:d + 1, :])
        a = jnp.exp(log_a)
        a_ref[d, 0] = a
        b_ref[d, 0] = jnp.sqrt(-jnp.tanh(log_a) * (a * a + 1.0)) * (ig * xc)


def _lru_gates(zs, conv_w, conv_b, wa_bd, ba, wx_bd, bx, lam, tc, tl):
    b, t, w = zs.shape
    tm = min(ROW_TILE, tc)
    prev_spec, next_spec = _halo_specs(tm, t, w)
    full = lambda shape: pl.BlockSpec(shape, lambda bi, i: (0,) * len(shape))
    drow = pl.BlockSpec((2, 1, tm, w), lambda bi, i: (0, bi, i, 0))
    dshp = jax.ShapeDtypeStruct((2, b, t, w), F32)
    return pl.pallas_call(
        functools.partial(_gates_kernel, tc=tc, tl=tl),
        grid=(b, t // tm),
        in_specs=[pl.BlockSpec((1, tm, w), lambda bi, i: (bi, i, 0)), prev_spec, next_spec,
                  full(conv_w.shape), full(conv_b.shape), full(wa_bd.shape), full(ba.shape),
                  full(wx_bd.shape), full(bx.shape), full(lam.shape)],
        out_specs=[drow, drow],
        out_shape=[dshp, dshp],
        compiler_params=_cparams(("parallel", "parallel")),
        name="gates",
    )(zs, zs, zs, conv_w, conv_b, wa_bd, ba, wx_bd, bx, lam)


def _lru_kernel(a_ref, b_ref, h_ref, carry_ref):
    d = pl.program_id(0)
    j = pl.program_id(1)

    @pl.when(j == 0)
    def _():
        carry_ref[...] = jnp.zeros_like(carry_ref)

    tt = a_ref.shape[1]

    def body(s, h):
        t = jnp.where(d == 0, s, tt - 1 - s)
        h = a_ref[0, t] * h + b_ref[0, t]
        h_ref[0, t] = h
        return h

    carry_ref[...] = lax.fori_loop(0, tt, body, carry_ref[...], unroll=8)


def _lru_scan(a_t, b_t, tc):
    _, t, b, w = a_t.shape
    tt = min(ROW_TILE, tc)
    nt = t // tt
    ntc = tc // tt
    tile = lambda d, j: jnp.where(d == 0, j, jnp.where(j < ntc, ntc - 1 - j, nt + ntc - 1 - j))
    spec = pl.BlockSpec((1, tt, b, w), lambda d, j: (d, tile(d, j), 0, 0))
    return pl.pallas_call(
        _lru_kernel,
        grid=(2, nt),
        in_specs=[spec, spec],
        out_specs=spec,
        out_shape=jax.ShapeDtypeStruct(a_t.shape, F32),
        scratch_shapes=[pltpu.VMEM((b, w), F32)],
        compiler_params=_cparams(("arbitrary", "arbitrary")),
        name="lru",
    )(a_t, b_t)


def _merge_kernel(y0_ref, y1_ref, r_ref, v_ref, ks_ref, g_ref, h0_ref, h1_ref, ggb_ref,
                  sga_ref, sgb_ref, x_ref, mod_ref, lng_ref, lnb_ref, rk_ref, e_ref,
                  wr_ref, wl_ref, wo_ref, g2_ref, rt_ref,
                  x1_ref, h2_ref, aff_ref, *, n_experts):
    e = e_ref[...]
    inv = 1.0 / HEAD_DIM
    y = y0_ref[0] + y1_ref[0]
    mean = _mm_exact_rhs(y, e) * inv
    yc = y - mean
    var = _mm_exact_rhs(yc * yc, e) * inv
    yn = yc * lax.rsqrt(var + LNX_EPS) * lng_ref[...] + lnb_ref[...]
    bonus = _mm_exact_rhs(r_ref[0] * ks_ref[0] * rk_ref[...], e) * v_ref[0]
    ya = _mm((yn + bonus) * g_ref[0], wr_ref[...])
    yb = _mm((h0_ref[0] + h1_ref[0]) * ggb_ref[0], wl_ref[...])
    mix = _mm(sga_ref[0] * ya + sgb_ref[0] * yb, wo_ref[...])
    gt1 = mod_ref[0, 0:1, :]
    sh2 = mod_ref[0, 1:2, :]
    sc2 = mod_ref[0, 2:3, :]
    x1 = x_ref[0] + gt1 * mix
    x1_ref[0] = x1
    xn = x1 * lax.rsqrt(jnp.mean(x1 * x1, axis=-1, keepdims=True) + RMS_EPS) * g2_ref[...]
    h2 = xn * (1.0 + sc2) + sh2
    h2_ref[0] = h2.astype(BF16)
    logits = _mm(h2, rt_ref[...])
    col = lax.broadcasted_iota(jnp.int32, logits.shape, 1)
    logits = jnp.where(col < n_experts, logits, -1e30)
    ex = jnp.exp(logits - jnp.max(logits, axis=-1, keepdims=True))
    aff = ex / jnp.sum(ex, axis=-1, keepdims=True)
    aff_ref[0] = jnp.transpose(aff)[0:n_experts, :]


def _merge(y0, y1, r, v, ks, g, h0, h1, ggb, sg, x, mod2, lnx_g, lnx_b, r_k, e_heads,
           wr, wl, wo, g2n, router_pad, tc, n_experts):
    b, tl, d = x.shape
    w = r.shape[-1]
    tm = min(ROW_TILE, tc)
    off = tc // tm
    seq = lambda width, blk=0: pl.BlockSpec((1, tm, width), lambda bi, i: (bi, i + off, blk))
    lat = lambda width: pl.BlockSpec((1, tm, width), lambda bi, i: (bi, i, 0))
    full = lambda shape: pl.BlockSpec(shape, lambda bi, i: (0,) * len(shape))
    return pl.pallas_call(
        functools.partial(_merge_kernel, n_experts=n_experts),
        grid=(b, tl // tm),
        in_specs=[seq(w), seq(w), seq(w), seq(w), seq(w), seq(w), lat(w), lat(w), seq(w),
                  seq(d, 0), seq(d, 1), lat(d),
                  pl.BlockSpec((1, 3, d), lambda bi, i: (bi, 0, 0)),
                  full((1, w)), full((1, w)), full((1, w)), full(e_heads.shape),
                  full(wr.shape), full(wl.shape), full(wo.shape), full((1, d)),
                  full(router_pad.shape)],
        out_specs=[lat(d), lat(d),
                   pl.BlockSpec((1, n_experts, tm), lambda bi, i: (bi, 0, i))],
        out_shape=[jax.ShapeDtypeStruct((b, tl, d), F32),
                   jax.ShapeDtypeStruct((b, tl, d), BF16),
                   jax.ShapeDtypeStruct((b, n_experts, tl), F32)],
        compiler_params=_cparams(("parallel", "parallel")),
        name="merge",
    )(y0, y1, r, v, ks, g, h0, h1, ggb, sg, sg, x, mod2, lnx_g, lnx_b, r_k, e_heads,
      wr, wl, wo, g2n, router_pad)


def _ffn_kernel(xs_ref, gate_ref, w1_ref, w3_ref, w2_ref, y_ref):
    xs = xs_ref[0, 0]
    a = jnp.dot(xs, w1_ref[0], preferred_element_type=F32)
    hid = (a * _sigmoid(a)) * jnp.dot(xs, w3_ref[0], preferred_element_type=F32)
    y_ref[0, 0] = jnp.dot(hid.astype(BF16), w2_ref[0], preferred_element_type=F32) * gate_ref[0, 0]


def _ffn(xs, gate, w1, w3, w2):
    b, ne, cap, d = xs.shape
    f = w1.shape[-1]
    tok = pl.BlockSpec((1, 1, cap, d), lambda e, bi: (bi, e, 0, 0))
    return pl.pallas_call(
        _ffn_kernel,
        grid=(ne, b),
        in_specs=[tok, pl.BlockSpec((1, 1, cap, 1), lambda e, bi: (bi, e, 0, 0)),
                  pl.BlockSpec((1, d, f), lambda e, bi: (e, 0, 0)),
                  pl.BlockSpec((1, d, f), lambda e, bi: (e, 0, 0)),
                  pl.BlockSpec((1, f, d), lambda e, bi: (e, 0, 0))],
        out_specs=tok,
        out_shape=jax.ShapeDtypeStruct((b, ne, cap, d), F32),
        compiler_params=_cparams(("parallel", "parallel")),
        name="ffn",
    )(xs, gate, w1, w3, w2)


def _final_kernel(x1_ref, moe_ref, gt_ref, g_ref, o_ref):
    x2 = x1_ref[0] + gt_ref[0] * moe_ref[0]
    o_ref[0] = x2 * lax.rsqrt(jnp.mean(x2 * x2, axis=-1, keepdims=True) + RMS_EPS) * g_ref[...]


def _final(x1, moe, gt2, final_g):
    b, tl, d = x1.shape
    tm = min(2 * ROW_TILE, tl)
    row = pl.BlockSpec((1, tm, d), lambda bi, i: (bi, i, 0))
    return pl.pallas_call(
        _final_kernel,
        grid=(b, tl // tm),
        in_specs=[row, row, pl.BlockSpec((1, 1, d), lambda bi, i: (bi, 0, 0)),
                  pl.BlockSpec((1, d), lambda bi, i: (0, 0))],
        out_specs=row,
        out_shape=jax.ShapeDtypeStruct((b, tl, d), F32),
        compiler_params=_cparams(("parallel", "parallel")),
        name="final",
    )(x1, moe, gt2, final_g)


def _block_diag(w):
    nb, n, _ = w.shape
    eye = jnp.eye(nb, dtype=w.dtype)
    return jnp.einsum("hij,hg->higj", w, eye).reshape(nb * n, nb * n)


def kernel(x, c, ctx, c_ctx, w_mod, b_mod, norm1_g, norm2_g, w_in, mu_rwkv, rwkv_w0, rwkv_w2, rwkv_a0, rwkv_a2, rwkv_g2, rwkv_k_k, rwkv_k_a, rwkv_r_k, lnx_g, lnx_b, w_proj_rwkv, conv_w, conv_b, lru_wa, lru_ba, lru_wx, lru_bx, lru_lam, w_proj_lru, w_out, router, moe_w1, moe_w3, moe_w2, final_g):
    depth = w_mod.shape[0]
    assert depth == 1, "single-layer configuration"
    b, tl, d = x.shape
    tc = ctx.shape[1]
    t = tc + tl
    w = rwkv_k_k.shape[-1]
    lw_ = lru_lam.shape[-1]
    rwkv_cols = mu_rwkv.shape[-1]
    n_experts = router.shape[-1]
    cap = EC_CAPACITY * tl // n_experts
    tm = min(ROW_TILE, tc)
    assert tc % tm == 0 and tl % tm == 0 and tm % CHUNK == 0
    layer = 0

    pad_rows = -(b + 1) % SUBLANES
    c_rows = jnp.concatenate([c, c_ctx[None], jnp.zeros((pad_rows, d), F32)], axis=0)
    m = _modulation(c_rows, w_mod[layer], b_mod[layer])
    m_lat = m[:b].reshape(b, 6, d)
    m_ctx = jnp.broadcast_to(m[b].reshape(1, 6, d), (b, 6, d))
    mod1 = jnp.stack([m_ctx[:, 0:2], m_lat[:, 0:2]], axis=1)
    mod2 = m_lat[:, 2:5]
    gt2 = m_lat[:, 5:6]

    xc = jnp.concatenate([ctx, x], axis=1)
    rw, xb, ggb, sg = _inproj(xc, norm1_g[layer][None], mod1, w_in[layer].astype(BF16),
                              tc, rwkv_cols, lw_)

    heads = w // HEAD_DIM
    e_heads = _block_diag(jnp.ones((heads, HEAD_DIM, HEAD_DIM), BF16))
    zpad = jnp.zeros((2, RWKV_W_RANK, w), F32)
    w2p = jnp.concatenate([rwkv_w2[layer], zpad], axis=1)
    a2p = jnp.concatenate([zpad, rwkv_a2[layer]], axis=1)
    r, v, kk, g, ks, lw, kd, bd = _rwkv_prep(
        rw, mu_rwkv[layer][None], rwkv_k_k[layer][None], rwkv_k_a[layer][None],
        rwkv_w0[layer], rwkv_a0[layer], w2p, a2p, rwkv_g2[layer], e_heads, tc, tl)
    ra, yv, ac, sc = _rwkv_chunk(r, v, kk, lw, kd, bd, tm)
    y0, y1 = _rwkv_carry(ra, yv, ac, sc, tc)

    rows = tl // GRID_W
    xb_lat = xb[:, tc:].reshape(b, rows, GRID_W, lw_).transpose(0, 2, 1, 3).reshape(b, tl, lw_)
    zs = jnp.concatenate([xb[:, :tc], xb_lat], axis=1)
    a_c, b_c = _lru_gates(zs, conv_w[layer], conv_b[layer],
                          jax.vmap(_block_diag)(lru_wa[layer]), lru_ba[layer],
                          jax.vmap(_block_diag)(lru_wx[layer]), lru_bx[layer],
                          lru_lam[layer], tc, tl)
    h = _lru_scan(a_c.transpose(0, 2, 1, 3), b_c.transpose(0, 2, 1, 3), tc)
    h_lat = h[:, tc:].reshape(2, GRID_W, rows, b, lw_).transpose(0, 3, 2, 1, 4).reshape(2, b, tl, lw_)

    router_pad = jnp.pad(router[layer], ((0, 0), (0, LANES - n_experts)))
    x1, h2, aff = _merge(y0, y1, r, v, ks, g, h_lat[0], h_lat[1], ggb, sg, x, mod2,
                         lnx_g[layer][None], lnx_b[layer][None], rwkv_r_k[layer].reshape(1, w),
                         e_heads, w_proj_rwkv[layer].astype(BF16), w_proj_lru[layer].astype(BF16),
                         w_out[layer].astype(BF16), norm2_g[layer][None], router_pad, tc, n_experts)

    gate, idx = lax.top_k(aff, cap)
    xs = jax.vmap(lambda hb, ib: hb[ib])(h2, idx)
    ye = _ffn(xs, gate[..., None], moe_w1[layer].astype(BF16), moe_w3[layer].astype(BF16),
              moe_w2[layer].astype(BF16))
    flat_idx = (idx + (jnp.arange(b, dtype=idx.dtype) * tl)[:, None, None]).reshape(-1)
    moe = jnp.zeros((b * tl, d), F32).at[flat_idx].add(ye.reshape(-1, d)).reshape(b, tl, d)

    return _final(x1, moe, gt2, final_g[None])
```

```python
import functools

import jax
import jax.numpy as jnp
from jax import lax
from jax.experimental import pallas as pl
from jax.experimental.pallas import tpu as pltpu

F32 = jnp.float32
BF16 = jnp.bfloat16

GRID_W = 64
HEAD_DIM = 64
RWKV_W_RANK = 64
RWKV_A_RANK = 64
RWKV_G_RANK = 128
CONV_W = 4
LRU_C = 8.0
EC_CAPACITY = 2
RMS_EPS = 1e-6
LNX_EPS = 64e-5
MIN_NORMAL_BITS = 0x00800000

LANES = 128
SUBLANES = 8
CHUNK = 64
ROW_TILE = 256
CHUNK_GROUP = 8
VMEM_LIMIT = 56 * 1024 * 1024


def _cparams(sem):
    return pltpu.CompilerParams(dimension_semantics=sem, vmem_limit_bytes=VMEM_LIMIT)


def _mm(a, b):
    return jnp.dot(a.astype(BF16), b.astype(BF16), preferred_element_type=F32)


def _mm_nt(a, b):
    return lax.dot_general(a.astype(BF16), b.astype(BF16), (((1,), (1,)), ((), ())),
                           preferred_element_type=F32)


def _mm_tn(a, b):
    return lax.dot_general(a.astype(BF16), b.astype(BF16), (((0,), (0,)), ((), ())),
                           preferred_element_type=F32)


def _split3(a):
    hi = a.astype(BF16)
    r1 = a - hi.astype(F32)
    mid = r1.astype(BF16)
    lo = (r1 - mid.astype(F32)).astype(BF16)
    return hi, mid, lo


def _mm_exact_rhs(a, b01):
    hi, mid, lo = _split3(a)
    d = lambda z: jnp.dot(z, b01, preferred_element_type=F32)
    return d(hi) + d(mid) + d(lo)


def _mm_exact_lhs(a01, b):
    hi, mid, lo = _split3(b)
    d = lambda z: jnp.dot(a01, z, preferred_element_type=F32)
    return d(hi) + d(mid) + d(lo)


def _mm3(a, b):
    ah = a.astype(BF16)
    al = (a - ah.astype(F32)).astype(BF16)
    bh = b.astype(BF16)
    bl = (b - bh.astype(F32)).astype(BF16)
    d = lambda x, y: jnp.dot(x, y, preferred_element_type=F32)
    return d(ah, bh) + d(al, bh) + d(ah, bl)


def _softplus(z):
    return jnp.maximum(z, 0.0) + jnp.log(1.0 + jnp.exp(-jnp.abs(z)))


def _sigmoid(z):
    return 1.0 / (1.0 + jnp.exp(-z))


def _mod_kernel(c_ref, w_ref, b_ref, o_ref):
    c = c_ref[...]
    s = c * _sigmoid(c)
    o_ref[...] = jnp.dot(s, w_ref[...], precision=lax.Precision.HIGHEST,
                         preferred_element_type=F32) + b_ref[...]


def _modulation(c_rows, w_mod, b_mod):
    rows, d = c_rows.shape
    n = w_mod.shape[1]
    tn = 1536
    return pl.pallas_call(
        _mod_kernel,
        grid=(n // tn,),
        in_specs=[pl.BlockSpec((rows, d), lambda j: (0, 0)),
                  pl.BlockSpec((d, tn), lambda j: (0, j)),
                  pl.BlockSpec((1, tn), lambda j: (0, j))],
        out_specs=pl.BlockSpec((rows, tn), lambda j: (0, j)),
        out_shape=jax.ShapeDtypeStruct((rows, n), F32),
        compiler_params=_cparams(("arbitrary",)),
        name="mod",
    )(c_rows, w_mod, b_mod.reshape(1, n))


def _inproj_kernel(x_ref, g_ref, mod_ref, w_ref, rw_ref, xb_ref, ggb_ref, sg_ref, *, cols):
    x = x_ref[0]
    xn = x * lax.rsqrt(jnp.mean(x * x, axis=-1, keepdims=True) + RMS_EPS) * g_ref[...]
    sh = mod_ref[0, 0, 0:1, :]
    sc = mod_ref[0, 0, 1:2, :]
    u = (xn * (1.0 + sc) + sh).astype(BF16)
    c_rw, c_xb, c_gb = cols
    d = lambda lo, hi: jnp.dot(u, w_ref[:, lo:hi], preferred_element_type=F32)
    rw_ref[0] = d(0, c_rw)
    xb_ref[0] = d(c_rw, c_xb)
    ggb_ref[0] = jax.nn.gelu(d(c_xb, c_gb))
    sg_ref[0] = _sigmoid(d(c_gb, w_ref.shape[1]))


def _inproj(xc, g1, mod1, w_in_bf, tc, rwkv_cols, lru_w):
    b, t, d = xc.shape
    n = w_in_bf.shape[1]
    tm = min(ROW_TILE, tc)
    ntc = tc // tm
    cols = (rwkv_cols, rwkv_cols + lru_w, rwkv_cols + 2 * lru_w)
    gate_cols = n - cols[2]
    row_spec = lambda w: pl.BlockSpec((1, tm, w), lambda bi, i: (bi, i, 0))
    return pl.pallas_call(
        functools.partial(_inproj_kernel, cols=cols),
        grid=(b, t // tm),
        in_specs=[row_spec(d),
                  pl.BlockSpec((1, d), lambda bi, i: (0, 0)),
                  pl.BlockSpec((1, 1, 2, d), lambda bi, i: (bi, jnp.where(i < ntc, 0, 1), 0, 0)),
                  pl.BlockSpec((d, n), lambda bi, i: (0, 0))],
        out_specs=[row_spec(rwkv_cols), row_spec(lru_w), row_spec(lru_w), row_spec(gate_cols)],
        out_shape=[jax.ShapeDtypeStruct((b, t, rwkv_cols), F32),
                   jax.ShapeDtypeStruct((b, t, lru_w), F32),
                   jax.ShapeDtypeStruct((b, t, lru_w), F32),
                   jax.ShapeDtypeStruct((b, t, gate_cols), F32)],
        compiler_params=_cparams(("parallel", "parallel")),
        name="inproj",
    )(xc, g1, mod1, w_in_bf)


def _shift_down(z, halo_prev, s):
    tm = z.shape[0]
    rolled = pltpu.roll(z, s, 0)
    hp = pltpu.roll(halo_prev, s, 0)
    r8 = lax.broadcasted_iota(jnp.int32, (SUBLANES, 1), 0)
    head = jnp.where(r8 < s, hp, rolled[0:SUBLANES])
    return jnp.concatenate([head, rolled[SUBLANES:tm]], axis=0)


def _shift_up(z, halo_next, s):
    tm = z.shape[0]
    rolled = pltpu.roll(z, tm - s, 0)
    hn = pltpu.roll(halo_next, SUBLANES - s, 0)
    r8 = lax.broadcasted_iota(jnp.int32, (SUBLANES, 1), 0)
    tail = jnp.where(r8 >= SUBLANES - s, hn, rolled[tm - SUBLANES:tm])
    return jnp.concatenate([rolled[0:tm - SUBLANES], tail], axis=0)


def _seg_pos(i, tm, tc, tl):
    grow = lax.broadcasted_iota(jnp.int32, (tm, 1), 0) + i * tm
    in_ctx = grow < tc
    pos = jnp.where(in_ctx, grow, grow - tc)
    seg_len = jnp.where(in_ctx, tc, tl)
    return pos, seg_len


def _halo_specs(tm, t, width):
    nb = tm // SUBLANES
    last = t // SUBLANES - 1
    prev = pl.BlockSpec((1, SUBLANES, width), lambda bi, i: (bi, jnp.maximum(i * nb - 1, 0), 0))
    nxt = pl.BlockSpec((1, SUBLANES, width), lambda bi, i: (bi, jnp.minimum((i + 1) * nb, last), 0))
    return prev, nxt


def _prep_kernel(rw_ref, prev_ref, next_ref, mu_ref, kkw_ref, ka_ref, w0_ref, a0_ref,
                 w2_ref, a2_ref, g2_ref, e_ref,
                 r_ref, v_ref, kk_ref, g_ref, ks_ref, lw_ref, kd_ref, bd_ref, *, tc, tl, width):
    i = pl.program_id(1)
    p = rw_ref[0]
    tm = p.shape[0]
    pos, seg_len = _seg_pos(i, tm, tc, tl)
    prev = jnp.where(pos >= 1, _shift_down(p, prev_ref[0], 1), 0.0)
    nxt = jnp.where(pos + 1 < seg_len, _shift_up(p, next_ref[0], 1), 0.0)
    pm = p + mu_ref[...] * (0.5 * (prev + nxt) - p)

    w = width
    r = pm[:, 0:w]
    k = pm[:, w:2 * w]
    v = pm[:, 2 * w:3 * w]
    wa = pm[:, 3 * w:3 * w + RWKV_W_RANK + RWKV_A_RANK]
    gd = pm[:, 3 * w + RWKV_W_RANK + RWKV_A_RANK:]

    kkr = k * kkw_ref[...]
    ss = _mm_exact_rhs(kkr * kkr, e_ref[...])
    kk = kkr / jnp.maximum(jnp.sqrt(ss), 1e-12)
    r_ref[0] = r
    v_ref[0] = v
    kk_ref[0] = kk
    g_ref[0] = _mm(_sigmoid(gd), g2_ref[...])

    twa = jnp.tanh(wa)
    ka = ka_ref[...]
    ksum = jnp.zeros_like(k)
    for d in range(2):
        wpre = w0_ref[d:d + 1, :] + _mm(twa, w2_ref[d])
        lw_ref[d, 0] = -jnp.exp(-_softplus(-wpre) - 0.5)
        eta = _sigmoid(a0_ref[d:d + 1, :] + _mm(wa, a2_ref[d]))
        kd = k * (1.0 + (eta - 1.0) * ka)
        kd_ref[d, 0] = kd
        bd_ref[d, 0] = kk * eta
        ksum = ksum + kd
    ks_ref[0] = ksum


def _rwkv_prep(rw, mu, k_k, k_a, w0, a0, w2p, a2p, g2, e_heads, tc, tl):
    b, t, cols = rw.shape
    w = k_k.shape[-1]
    tm = min(ROW_TILE, tc)
    prev_spec, next_spec = _halo_specs(tm, t, cols)
    full = lambda shape: pl.BlockSpec(shape, lambda bi, i: (0,) * len(shape))
    row = pl.BlockSpec((1, tm, w), lambda bi, i: (bi, i, 0))
    drow = pl.BlockSpec((2, 1, tm, w), lambda bi, i: (0, bi, i, 0))
    shp = jax.ShapeDtypeStruct((b, t, w), F32)
    dshp = jax.ShapeDtypeStruct((2, b, t, w), F32)
    return pl.pallas_call(
        functools.partial(_prep_kernel, tc=tc, tl=tl, width=w),
        grid=(b, t // tm),
        in_specs=[pl.BlockSpec((1, tm, cols), lambda bi, i: (bi, i, 0)), prev_spec, next_spec,
                  full((1, cols)), full((1, w)), full((1, w)), full((2, w)), full((2, w)),
                  full(w2p.shape), full(a2p.shape), full(g2.shape), full(e_heads.shape)],
        out_specs=[row, row, row, row, row, drow, drow, drow],
        out_shape=[shp, shp, shp, shp, shp, dshp, dshp, dshp],
        compiler_params=_cparams(("parallel", "parallel")),
        name="prep",
    )(rw, rw, rw, mu, k_k, k_a, w0, a0, w2p, a2p, g2, e_heads)


def _chunk_group(units, m, nsq):
    c = CHUNK
    n = 2 * c
    m0, m1 = m["m0"], m["m1"]
    stack = lambda z: jnp.concatenate([z * m0, z * m1], axis=0)
    dup = lambda z: jnp.concatenate([z, z], axis=0)

    lcs = [_mm_exact_lhs(m["cum"], u[3]) for u in units]
    pre = []
    for (r, v, kk, lw, kd, bd), lc in zip(units, lcs):
        ltot = jnp.sum(lw, axis=0, keepdims=True)
        en = jnp.exp(-lc)
        eh = jnp.exp(ltot - lc)
        qa_bd = stack(-kk * jnp.exp(lc - lw))
        qr_bd = stack(r * jnp.exp(lc))
        pre.append(dict(q=jnp.concatenate([qa_bd, qr_bd], axis=0), qa=qa_bd, qr=qr_bd,
                        kb2=dup(bd * en), kx2=dup(kd * en), v=stack(v),
                        kbh=stack(bd * eh), kxh=stack(kd * eh), dg=m["eye"] * jnp.exp(ltot)))

    gb = [_mm_nt(p["q"], p["kb2"]) for p in pre]
    gx = [_mm_nt(p["q"], p["kx2"]) for p in pre]
    gab = [jnp.where(m["strict"], g[:n], 0.0) for g in gb]
    grb = [jnp.where(m["incl"], g[n:], 0.0) for g in gb]
    gak = [jnp.where(m["strict"], g[:n], 0.0) for g in gx]
    grk = [jnp.where(m["incl"], g[n:], 0.0) for g in gx]

    t_inv = [m["eye"] + g for g in gab]
    gp = [_mm(g, g) for g in gab]
    for _ in range(nsq - 1):
        prod = [_mm(jnp.concatenate([t, g], axis=0), g) for t, g in zip(t_inv, gp)]
        t_inv = [t + p[:n] for t, p in zip(t_inv, prod)]
        gp = [p[n:] for p in prod]
    t_inv = [t + _mm(t, g) for t, g in zip(t_inv, gp)]

    gv = [_mm(g, p["v"]) for g, p in zip(gak, pre)]
    tw = [_mm(t, jnp.concatenate([p["qa"], x], axis=1)) for t, p, x in zip(t_inv, pre, gv)]
    gw = [_mm(g, x) for g, x in zip(grb, tw)]
    gkv = [_mm(g, p["v"]) for g, p in zip(grk, pre)]
    kw = [_mm_tn(p["kbh"], x) for p, x in zip(pre, tw)]
    kxv = [_mm_tn(p["kxh"], p["v"]) for p in pre]

    fold = lambda z: z[0:c] + z[c:2 * c]
    outs = []
    for p, g, gk, k, kx in zip(pre, gw, gkv, kw, kxv):
        ra = p["qr"] + g[:, :n]
        yv = g[:, n:] + gk
        ac = p["dg"] + k[:, :n]
        sc = k[:, n:] + kx
        outs.append((fold(ra), fold(yv), fold(ac), fold(sc)))
    return outs


def _chunk_masks(rev):
    c = CHUNK
    n = 2 * c
    ti = lax.broadcasted_iota(jnp.int32, (c, c), 0)
    ii = lax.broadcasted_iota(jnp.int32, (c, c), 1)
    sign = jnp.where(rev, -1, 1)
    cum = jnp.where((ii - ti) * sign <= 0, 1.0, 0.0).astype(BF16)
    row = lax.broadcasted_iota(jnp.int32, (n, n), 0)
    col = lax.broadcasted_iota(jnp.int32, (n, n), 1)
    same = (row // c) == (col // c)
    tt = row % c
    it = col % c
    before = (it - tt) * sign < 0
    lane = lax.broadcasted_iota(jnp.int32, (1, n), 1)
    return {
        "cum": cum,
        "strict": same & before,
        "incl": same & ((it - tt) * sign <= 0),
        "eye": jnp.where(row == col, 1.0, 0.0).astype(F32),
        "m0": jnp.where(lane < c, 1.0, 0.0).astype(F32),
        "m1": jnp.where(lane >= c, 1.0, 0.0).astype(F32),
    }


def _chunk_kernel(r_ref, v_ref, kk_ref, lw_ref, kd_ref, bd_ref,
                  ra_ref, yv_ref, ac_ref, sc_ref, *, nsq):
    rev = pl.program_id(0) == 1
    m = _chunk_masks(rev)
    tm, w = r_ref.shape[1], r_ref.shape[2]
    n = 2 * CHUNK
    nhp = w // n
    cpg = max(1, CHUNK_GROUP // nhp)
    for c0 in range(0, tm // CHUNK, cpg):
        keys, units = [], []
        for ci in range(c0, min(c0 + cpg, tm // CHUNK)):
            rows = slice(ci * CHUNK, (ci + 1) * CHUNK)
            for hp in range(nhp):
                lanes = slice(hp * n, (hp + 1) * n)
                keys.append((ci, hp, rows, lanes))
                units.append((r_ref[0, rows, lanes], v_ref[0, rows, lanes], kk_ref[0, rows, lanes],
                              lw_ref[0, 0, rows, lanes], kd_ref[0, 0, rows, lanes],
                              bd_ref[0, 0, rows, lanes]))
        for (ci, hp, rows, lanes), (ra, yv, ac, sc) in zip(keys, _chunk_group(units, m, nsq)):
            ra_ref[0, 0, rows, lanes] = ra
            yv_ref[0, 0, rows, lanes] = yv
            ac_ref[0, 0, ci, hp] = ac
            sc_ref[0, 0, ci, hp] = sc


def _rwkv_chunk(r, v, kk, lw, kd, bd, tm):
    b, t, w = r.shape
    n = 2 * CHUNK
    nc = t // CHUNK
    cpt = tm // CHUNK
    nsq = (CHUNK - 1).bit_length() - 1
    row = pl.BlockSpec((1, tm, w), lambda d, bi, i: (bi, i, 0))
    drow = pl.BlockSpec((1, 1, tm, w), lambda d, bi, i: (d, bi, i, 0))
    mat = pl.BlockSpec((1, 1, cpt, w // n, CHUNK, n), lambda d, bi, i: (d, bi, i, 0, 0, 0))
    dshp = jax.ShapeDtypeStruct((2, b, t, w), F32)
    mshp = jax.ShapeDtypeStruct((2, b, nc, w // n, CHUNK, n), F32)
    return pl.pallas_call(
        functools.partial(_chunk_kernel, nsq=nsq),
        grid=(2, b, t // tm),
        in_specs=[row, row, row, drow, drow, drow],
        out_specs=[drow, drow, mat, mat],
        out_shape=[dshp, dshp, mshp, mshp],
        compiler_params=_cparams(("parallel", "parallel", "parallel")),
        name="chunk",
    )(r, v, kk, lw, kd, bd)


def _carry_kernel(ra0, yv0, ac0, sc0, ra1, yv1, ac1, sc1, y0_ref, y1_ref, st_ref):
    j = pl.program_id(0)

    @pl.when(j == 0)
    def _():
        st_ref[...] = jnp.zeros_like(st_ref)

    nb, c, w = y0_ref.shape
    n = 2 * c
    lane = lax.broadcasted_iota(jnp.int32, (1, n), 1)
    m0 = jnp.where(lane < c, 1.0, 0.0).astype(F32)
    m1 = 1.0 - m0
    stack = lambda z: jnp.concatenate([z * m0, z * m1], axis=0)

    for d, (ra, yv, ac, sc, y_ref) in enumerate(((ra0, yv0, ac0, sc0, y0_ref),
                                                 (ra1, yv1, ac1, sc1, y1_ref))):
        def body(bi, carry, d=d, ra=ra, yv=yv, ac=ac, sc=sc, y_ref=y_ref):
            for hp in range(w // n):
                lanes = slice(hp * n, (hp + 1) * n)
                st = st_ref[d, bi, hp]
                y_ref[bi, :, lanes] = _mm3(ra[0, bi, :, lanes], st) + yv[0, bi, :, lanes]
                st_ref[d, bi, hp] = _mm3(stack(ac[0, bi, 0, hp]), st) + stack(sc[0, bi, 0, hp])
            return carry
        lax.fori_loop(0, nb, body, 0)


def _rwkv_carry(ra, yv, ac, sc, tc):
    _, b, t, w = ra.shape
    n = 2 * CHUNK
    nc = t // CHUNK
    ncc = tc // CHUNK
    fwd = lambda j: j
    bwd = lambda j: jnp.where(j < ncc, ncc - 1 - j, nc + ncc - 1 - j)
    row = lambda d, f: pl.BlockSpec((1, b, CHUNK, w), lambda j: (d, 0, f(j), 0))
    mat = lambda d, f: pl.BlockSpec((1, b, 1, w // n, CHUNK, n), lambda j: (d, 0, f(j), 0, 0, 0))
    out = lambda f: pl.BlockSpec((b, CHUNK, w), lambda j: (0, f(j), 0))
    shp = jax.ShapeDtypeStruct((b, t, w), F32)
    return pl.pallas_call(
        _carry_kernel,
        grid=(nc,),
        in_specs=[row(0, fwd), row(0, fwd), mat(0, fwd), mat(0, fwd),
                  row(1, bwd), row(1, bwd), mat(1, bwd), mat(1, bwd)],
        out_specs=[out(fwd), out(bwd)],
        out_shape=[shp, shp],
        scratch_shapes=[pltpu.VMEM((2, b, w // n, n, n), F32)],
        compiler_params=_cparams(("arbitrary",)),
        name="carry",
    )(ra, yv, ac, sc, ra, yv, ac, sc)


def _gates_kernel(z_ref, prev_ref, next_ref, cw_ref, cb_ref, wa_ref, ba_ref, wx_ref, bx_ref,
                  lam_ref, a_ref, b_ref, *, tc, tl):
    i = pl.program_id(1)
    z = z_ref[0]
    tm = z.shape[0]
    pos, seg_len = _seg_pos(i, tm, tc, tl)
    for d in range(2):
        xc = cb_ref[d:d + 1, :] + cw_ref[d, CONV_W - 1:CONV_W, :] * z
        for s in range(1, CONV_W):
            if d == 0:
                zs = jnp.where(pos >= s, _shift_down(z, prev_ref[0], s), 0.0)
            else:
                zs = jnp.where(pos + s < seg_len, _shift_up(z, next_ref[0], s), 0.0)
            xc = xc + cw_ref[d, CONV_W - 1 - s:CONV_W - s, :] * zs
        rg = _sigmoid(_mm(xc, wa_ref[d]) + ba_ref[d:d + 1, :])
        ig = _sigmoid(_mm(xc, wx_ref[d]) + bx_ref[d:d + 1, :])
        log_a = -LRU_C * rg * _softplus(-lam_ref[d:d + 1, :])
        a = jnp.exp(log_a)
        a_ref[d, 0] = a
        b_ref[d, 0] = jnp.sqrt(-jnp.tanh(log_a) * (a * a + 1.0)) * (ig * xc)


def _lru_gates(zs, conv_w, conv_b, wa_bd, ba, wx_bd, bx, lam, tc, tl):
    b, t, w = zs.shape
    tm = min(ROW_TILE, tc)
    prev_spec, next_spec = _halo_specs(tm, t, w)
    full = lambda shape: pl.BlockSpec(shape, lambda bi, i: (0,) * len(shape))
    drow = pl.BlockSpec((2, 1, tm, w), lambda bi, i: (0, bi, i, 0))
    dshp = jax.ShapeDtypeStruct((2, b, t, w), F32)
    return pl.pallas_call(
        functools.partial(_gates_kernel, tc=tc, tl=tl),
        grid=(b, t // tm),
        in_specs=[pl.BlockSpec((1, tm, w), lambda bi, i: (bi, i, 0)), prev_spec, next_spec,
                  full(conv_w.shape), full(conv_b.shape), full(wa_bd.shape), full(ba.shape),
                  full(wx_bd.shape), full(bx.shape), full(lam.shape)],
        out_specs=[drow, drow],
        out_shape=[dshp, dshp],
        compiler_params=_cparams(("parallel", "parallel")),
        name="gates",
    )(zs, zs, zs, conv_w, conv_b, wa_bd, ba, wx_bd, bx, lam)


def _lru_kernel(a_ref, b_ref, h_ref, carry_ref):
    d = pl.program_id(0)
    j = pl.program_id(1)

    @pl.when(j == 0)
    def _():
        carry_ref[...] = jnp.zeros_like(carry_ref)

    tt = a_ref.shape[1]

    def body(s, h):
        t = jnp.where(d == 0, s, tt - 1 - s)
        h = a_ref[0, t] * h + b_ref[0, t]
        h_ref[0, t] = h
        return h

    carry_ref[...] = lax.fori_loop(0, tt, body, carry_ref[...], unroll=8)


def _lru_scan(a_t, b_t, tc):
    _, t, b, w = a_t.shape
    tt = min(ROW_TILE, tc)
    nt = t // tt
    ntc = tc // tt
    tile = lambda d, j: jnp.where(d == 0, j, jnp.where(j < ntc, ntc - 1 - j, nt + ntc - 1 - j))
    spec = pl.BlockSpec((1, tt, b, w), lambda d, j: (d, tile(d, j), 0, 0))
    return pl.pallas_call(
        _lru_kernel,
        grid=(2, nt),
        in_specs=[spec, spec],
        out_specs=spec,
        out_shape=jax.ShapeDtypeStruct(a_t.shape, F32),
        scratch_shapes=[pltpu.VMEM((b, w), F32)],
        compiler_params=_cparams(("arbitrary", "arbitrary")),
        name="lru",
    )(a_t, b_t)


def _merge_kernel(y0_ref, y1_ref, r_ref, v_ref, ks_ref, g_ref, h0_ref, h1_ref, ggb_ref,
                  sga_ref, sgb_ref, x_ref, mod_ref, lng_ref, lnb_ref, rk_ref, e_ref,
                  wr_ref, wl_ref, wo_ref, g2_ref, rt_ref,
                  x1_ref, h2_ref, aff_ref, *, n_experts):
    e = e_ref[...]
    inv = 1.0 / HEAD_DIM
    y = y0_ref[0] + y1_ref[0]
    mean = _mm_exact_rhs(y, e) * inv
    yc = y - mean
    var = _mm_exact_rhs(yc * yc, e) * inv
    yn = yc * lax.rsqrt(var + LNX_EPS) * lng_ref[...] + lnb_ref[...]
    bonus = _mm_exact_rhs(r_ref[0] * ks_ref[0] * rk_ref[...], e) * v_ref[0]
    ya = _mm((yn + bonus) * g_ref[0], wr_ref[...])
    yb = _mm((h0_ref[0] + h1_ref[0]) * ggb_ref[0], wl_ref[...])
    mix = _mm(sga_ref[0] * ya + sgb_ref[0] * yb, wo_ref[...])
    gt1 = mod_ref[0, 0:1, :]
    sh2 = mod_ref[0, 1:2, :]
    sc2 = mod_ref[0, 2:3, :]
    x1 = x_ref[0] + gt1 * mix
    x1_ref[0] = x1
    xn = x1 * lax.rsqrt(jnp.mean(x1 * x1, axis=-1, keepdims=True) + RMS_EPS) * g2_ref[...]
    h2 = xn * (1.0 + sc2) + sh2
    h2_ref[0] = h2.astype(BF16)
    logits = _mm(h2, rt_ref[...])
    col = lax.broadcasted_iota(jnp.int32, logits.shape, 1)
    logits = jnp.where(col < n_experts, logits, -1e30)
    ex = jnp.exp(logits - jnp.max(logits, axis=-1, keepdims=True))
    aff = ex / jnp.sum(ex, axis=-1, keepdims=True)
    aff_ref[0] = jnp.transpose(aff)[0:n_experts, :]


def _merge(y0, y1, r, v, ks, g, h0, h1, ggb, sg, x, mod2, lnx_g, lnx_b, r_k, e_heads,
           wr, wl, wo, g2n, router_pad, tc, n_experts):
    b, tl, d = x.shape
    w = r.shape[-1]
    tm = min(ROW_TILE, tc)
    off = tc // tm
    seq = lambda width, blk=0: pl.BlockSpec((1, tm, width), lambda bi, i: (bi, i + off, blk))
    lat = lambda width: pl.BlockSpec((1, tm, width), lambda bi, i: (bi, i, 0))
    full = lambda shape: pl.BlockSpec(shape, lambda bi, i: (0,) * len(shape))
    return pl.pallas_call(
        functools.partial(_merge_kernel, n_experts=n_experts),
        grid=(b, tl // tm),
        in_specs=[seq(w), seq(w), seq(w), seq(w), seq(w), seq(w), lat(w), lat(w), seq(w),
                  seq(d, 0), seq(d, 1), lat(d),
                  pl.BlockSpec((1, 3, d), lambda bi, i: (bi, 0, 0)),
                  full((1, w)), full((1, w)), full((1, w)), full(e_heads.shape),
                  full(wr.shape), full(wl.shape), full(wo.shape), full((1, d)),
                  full(router_pad.shape)],
        out_specs=[lat(d), lat(d),
                   pl.BlockSpec((1, n_experts, tm), lambda bi, i: (bi, 0, i))],
        out_shape=[jax.ShapeDtypeStruct((b, tl, d), F32),
                   jax.ShapeDtypeStruct((b, tl, d), BF16),
                   jax.ShapeDtypeStruct((b, n_experts, tl), F32)],
        compiler_params=_cparams(("parallel", "parallel")),
        name="merge",
    )(y0, y1, r, v, ks, g, h0, h1, ggb, sg, sg, x, mod2, lnx_g, lnx_b, r_k, e_heads,
      wr, wl, wo, g2n, router_pad)


def _prefix_incl(x, lane):
    n = x.shape[-1]
    s = 1
    while s < n:
        x = x + jnp.where(lane >= s, pltpu.roll(x, s, 1), 0.0)
        s *= 2
    return x


def _route_kernel(aff_ref, slot_ref, *, cap):
    aff = aff_ref[0]
    ne, t = aff.shape
    count_ge = lambda v: jnp.sum(jnp.where(aff >= v, 1.0, 0.0), axis=-1, keepdims=True)

    def bit_step(i, thr):
        cand = thr | jnp.left_shift(jnp.int32(1), 30 - i)
        return jnp.where(count_ge(pltpu.bitcast(cand, F32)) >= cap, cand, thr)

    thr = lax.fori_loop(0, 31, bit_step, jnp.zeros((ne, 1), jnp.int32))
    lo = pltpu.bitcast(thr, F32)
    hi = pltpu.bitcast(jnp.maximum(thr + 1, MIN_NORMAL_BITS), F32)

    def halve(_, lh):
        lo, hi = lh
        mid = lo + 0.5 * (hi - lo)
        ge = count_ge(mid) >= cap
        return jnp.where(ge, mid, lo), jnp.where(ge, hi, mid)

    lo, hi = lax.fori_loop(0, 30, halve, (lo, hi))
    gt = aff >= hi
    eq = jnp.where((aff >= lo) & (aff < hi), 1.0, 0.0)
    need = cap - jnp.sum(jnp.where(gt, 1.0, 0.0), axis=-1, keepdims=True)
    lane = lax.broadcasted_iota(jnp.int32, (ne, t), 1)
    eq_rank = _prefix_incl(eq, lane) - eq
    sel = jnp.where(gt, 1.0, jnp.where(eq_rank < need, eq, 0.0))
    slot = _prefix_incl(sel, lane) - sel
    slot_ref[0] = jnp.where(sel > 0.0, slot, -1.0)


def _route(aff, cap):
    b, ne, t = aff.shape
    spec = pl.BlockSpec((1, ne, t), lambda bi: (bi, 0, 0))
    return pl.pallas_call(
        functools.partial(_route_kernel, cap=cap),
        grid=(b,),
        in_specs=[spec],
        out_specs=spec,
        out_shape=jax.ShapeDtypeStruct((b, ne, t), F32),
        compiler_params=_cparams(("parallel",)),
        name="route",
    )(aff)


def _ffn_kernel(slot_ref, aff_ref, h2_ref, w1_ref, w3_ref, w2_ref, y_ref, *, cap):
    slots = lax.broadcasted_iota(jnp.int32, (cap, 1), 0).astype(F32)
    hit = slot_ref[0, 0] == slots
    gate = jnp.sum(jnp.where(hit, aff_ref[0, 0], 0.0), axis=-1, keepdims=True)
    onehot = jnp.where(hit, 1.0, 0.0).astype(BF16)
    xs = jnp.dot(onehot, h2_ref[0], preferred_element_type=F32).astype(BF16)
    a = jnp.dot(xs, w1_ref[0], preferred_element_type=F32)
    hid = (a * _sigmoid(a)) * jnp.dot(xs, w3_ref[0], preferred_element_type=F32)
    y = jnp.dot(hid.astype(BF16), w2_ref[0], preferred_element_type=F32) * gate
    y_ref[0, 0] = y.astype(BF16)


def _ffn(slot4, aff4, h2, w1, w3, w2, cap):
    b, ne, _, t = slot4.shape
    d = h2.shape[-1]
    f = w1.shape[-1]
    row = pl.BlockSpec((1, 1, 1, t), lambda bi, e: (bi, e, 0, 0))
    return pl.pallas_call(
        functools.partial(_ffn_kernel, cap=cap),
        grid=(b, ne),
        in_specs=[row, row,
                  pl.BlockSpec((1, t, d), lambda bi, e: (bi, 0, 0)),
                  pl.BlockSpec((1, d, f), lambda bi, e: (e, 0, 0)),
                  pl.BlockSpec((1, d, f), lambda bi, e: (e, 0, 0)),
                  pl.BlockSpec((1, f, d), lambda bi, e: (e, 0, 0))],
        out_specs=pl.BlockSpec((1, 1, cap, d), lambda bi, e: (bi, e, 0, 0)),
        out_shape=jax.ShapeDtypeStruct((b, ne, cap, d), BF16),
        compiler_params=_cparams(("parallel", "arbitrary")),
        name="ffn",
    )(slot4, aff4, h2, w1, w3, w2)


def _combine_kernel(slot_ref, y_ref, x1_ref, gt_ref, g_ref, o_ref, acc_ref, *, cap):
    e = pl.program_id(2)

    @pl.when(e == 0)
    def _():
        acc_ref[...] = jnp.zeros_like(acc_ref)

    slots = lax.broadcasted_iota(jnp.int32, (cap, 1), 0).astype(F32)
    onehot = jnp.where(slot_ref[0, 0] == slots, 1.0, 0.0).astype(BF16)
    acc_ref[...] += lax.dot_general(onehot, y_ref[0, 0], (((0,), (0,)), ((), ())),
                                    preferred_element_type=F32)

    @pl.when(e == pl.num_programs(2) - 1)
    def _():
        x2 = x1_ref[0] + gt_ref[0] * acc_ref[...]
        o_ref[0] = x2 * lax.rsqrt(jnp.mean(x2 * x2, axis=-1, keepdims=True) + RMS_EPS) * g_ref[...]


def _combine(slot4, ye, x1, gt2, final_g, cap):
    b, tl, d = x1.shape
    ne = slot4.shape[1]
    tt = min(4 * ROW_TILE, tl)
    row = pl.BlockSpec((1, tt, d), lambda bi, j, e: (bi, j, 0))
    return pl.pallas_call(
        functools.partial(_combine_kernel, cap=cap),
        grid=(b, tl // tt, ne),
        in_specs=[pl.BlockSpec((1, 1, 1, tt), lambda bi, j, e: (bi, e, 0, j)),
                  pl.BlockSpec((1, 1, cap, d), lambda bi, j, e: (bi, e, 0, 0)),
                  row,
                  pl.BlockSpec((1, 1, d), lambda bi, j, e: (bi, 0, 0)),
                  pl.BlockSpec((1, d), lambda bi, j, e: (0, 0))],
        out_specs=row,
        out_shape=jax.ShapeDtypeStruct((b, tl, d), F32),
        scratch_shapes=[pltpu.VMEM((tt, d), F32)],
        compiler_params=_cparams(("parallel", "parallel", "arbitrary")),
        name="combine",
    )(slot4, ye, x1, gt2, final_g)


def _block_diag(w):
    nb, n, _ = w.shape
    eye = jnp.eye(nb, dtype=w.dtype)
    return jnp.einsum("hij,hg->higj", w, eye).reshape(nb * n, nb * n)


def kernel(x, c, ctx, c_ctx, w_mod, b_mod, norm1_g, norm2_g, w_in, mu_rwkv, rwkv_w0, rwkv_w2, rwkv_a0, rwkv_a2, rwkv_g2, rwkv_k_k, rwkv_k_a, rwkv_r_k, lnx_g, lnx_b, w_proj_rwkv, conv_w, conv_b, lru_wa, lru_ba, lru_wx, lru_bx, lru_lam, w_proj_lru, w_out, router, moe_w1, moe_w3, moe_w2, final_g):
    depth = w_mod.shape[0]
    assert depth == 1, "single-layer configuration"
    b, tl, d = x.shape
    tc = ctx.shape[1]
    t = tc + tl
    w = rwkv_k_k.shape[-1]
    lw_ = lru_lam.shape[-1]
    rwkv_cols = mu_rwkv.shape[-1]
    n_experts = router.shape[-1]
    cap = EC_CAPACITY * tl // n_experts
    tm = min(ROW_TILE, tc)
    assert tc % tm == 0 and tl % tm == 0 and tm % CHUNK == 0
    layer = 0

    pad_rows = -(b + 1) % SUBLANES
    c_rows = jnp.concatenate([c, c_ctx[None], jnp.zeros((pad_rows, d), F32)], axis=0)
    m = _modulation(c_rows, w_mod[layer], b_mod[layer])
    m_lat = m[:b].reshape(b, 6, d)
    m_ctx = jnp.broadcast_to(m[b].reshape(1, 6, d), (b, 6, d))
    mod1 = jnp.stack([m_ctx[:, 0:2], m_lat[:, 0:2]], axis=1)
    mod2 = m_lat[:, 2:5]
    gt2 = m_lat[:, 5:6]

    xc = jnp.concatenate([ctx, x], axis=1)
    rw, xb, ggb, sg = _inproj(xc, norm1_g[layer][None], mod1, w_in[layer].astype(BF16),
                              tc, rwkv_cols, lw_)

    heads = w // HEAD_DIM
    e_heads = _block_diag(jnp.ones((heads, HEAD_DIM, HEAD_DIM), BF16))
    zpad = jnp.zeros((2, RWKV_W_RANK, w), F32)
    w2p = jnp.concatenate([rwkv_w2[layer], zpad], axis=1)
    a2p = jnp.concatenate([zpad, rwkv_a2[layer]], axis=1)
    r, v, kk, g, ks, lw, kd, bd = _rwkv_prep(
        rw, mu_rwkv[layer][None], rwkv_k_k[layer][None], rwkv_k_a[layer][None],
        rwkv_w0[layer], rwkv_a0[layer], w2p, a2p, rwkv_g2[layer], e_heads, tc, tl)
    ra, yv, ac, sc = _rwkv_chunk(r, v, kk, lw, kd, bd, tm)
    y0, y1 = _rwkv_carry(ra, yv, ac, sc, tc)

    rows = tl // GRID_W
    xb_lat = xb[:, tc:].reshape(b, rows, GRID_W, lw_).transpose(0, 2, 1, 3).reshape(b, tl, lw_)
    zs = jnp.concatenate([xb[:, :tc], xb_lat], axis=1)
    a_c, b_c = _lru_gates(zs, conv_w[layer], conv_b[layer],
                          jax.vmap(_block_diag)(lru_wa[layer]), lru_ba[layer],
                          jax.vmap(_block_diag)(lru_wx[layer]), lru_bx[layer],
                          lru_lam[layer], tc, tl)
    h = _lru_scan(a_c.transpose(0, 2, 1, 3), b_c.transpose(0, 2, 1, 3), tc)
    h_lat = h[:, tc:].reshape(2, GRID_W, rows, b, lw_).transpose(0, 3, 2, 1, 4).reshape(2, b, tl, lw_)

    router_pad = jnp.pad(router[layer], ((0, 0), (0, LANES - n_experts)))
    x1, h2, aff = _merge(y0, y1, r, v, ks, g, h_lat[0], h_lat[1], ggb, sg, x, mod2,
                         lnx_g[layer][None], lnx_b[layer][None], rwkv_r_k[layer].reshape(1, w),
                         e_heads, w_proj_rwkv[layer].astype(BF16), w_proj_lru[layer].astype(BF16),
                         w_out[layer].astype(BF16), norm2_g[layer][None], router_pad, tc, n_experts)

    slot4 = _route(aff, cap).reshape(b, n_experts, 1, tl)
    ye = _ffn(slot4, aff.reshape(b, n_experts, 1, tl), h2, moe_w1[layer].astype(BF16),
              moe_w3[layer].astype(BF16), moe_w2[layer].astype(BF16), cap)
    return _combine(slot4, ye, x1, gt2, final_g[None], cap)
```

```python
import functools

import jax
import jax.numpy as jnp
from jax import lax
from jax.experimental import pallas as pl
from jax.experimental.pallas import tpu as pltpu

F32 = jnp.float32
BF16 = jnp.bfloat16

GRID_W = 64
HEAD_DIM = 64
RWKV_W_RANK = 64
RWKV_A_RANK = 64
RWKV_G_RANK = 128
CONV_W = 4
LRU_C = 8.0
EC_CAPACITY = 2
RMS_EPS = 1e-6
LNX_EPS = 64e-5
MIN_NORMAL_BITS = 0x00800000
MIN_NORMAL = 1.1754944e-38

LANES = 128
SUBLANES = 8
CHUNK = 64
ROW_TILE = 256
VMEM_LIMIT = 56 * 1024 * 1024


def _cparams(sem):
    return pltpu.CompilerParams(dimension_semantics=sem, vmem_limit_bytes=VMEM_LIMIT)


def _mm(a, b):
    return jnp.dot(a.astype(BF16), b.astype(BF16), preferred_element_type=F32)


def _mm_nt(a, b):
    return lax.dot_general(a.astype(BF16), b.astype(BF16), (((1,), (1,)), ((), ())),
                           preferred_element_type=F32)


def _split3(a):
    hi = a.astype(BF16)
    r1 = a - hi.astype(F32)
    mid = r1.astype(BF16)
    lo = (r1 - mid.astype(F32)).astype(BF16)
    return hi, mid, lo


def _mm_split_rhs(a, b01):
    hi = a.astype(BF16)
    lo = (a - hi.astype(F32)).astype(BF16)
    d = lambda z: jnp.dot(z, b01, preferred_element_type=F32)
    return d(hi) + d(lo)


def _mm_exact_lhs(a01, b):
    hi, mid, lo = _split3(b)
    d = lambda z: jnp.dot(a01, z, preferred_element_type=F32)
    return d(hi) + d(mid) + d(lo)


def _mm3(a, b):
    ah = a.astype(BF16)
    al = (a - ah.astype(F32)).astype(BF16)
    bh = b.astype(BF16)
    bl = (b - bh.astype(F32)).astype(BF16)
    d = lambda x, y: jnp.dot(x, y, preferred_element_type=F32)
    return d(ah, bh) + d(al, bh) + d(ah, bl)


def _softplus(z):
    return jnp.maximum(z, 0.0) + jnp.log(1.0 + jnp.exp(-jnp.abs(z)))


def _sigmoid(z):
    return 1.0 / (1.0 + jnp.exp(-z))


def _mod_kernel(c_ref, w_ref, b_ref, o_ref):
    c = c_ref[...]
    s = c * _sigmoid(c)
    o_ref[...] = jnp.dot(s, w_ref[...], precision=lax.Precision.HIGHEST,
                         preferred_element_type=F32) + b_ref[...]


def _modulation(c_rows, w_mod, b_mod):
    rows, d = c_rows.shape
    n = w_mod.shape[1]
    tn = 1536
    return pl.pallas_call(
        _mod_kernel,
        grid=(n // tn,),
        in_specs=[pl.BlockSpec((rows, d), lambda j: (0, 0)),
                  pl.BlockSpec((d, tn), lambda j: (0, j)),
                  pl.BlockSpec((1, tn), lambda j: (0, j))],
        out_specs=pl.BlockSpec((rows, tn), lambda j: (0, j)),
        out_shape=jax.ShapeDtypeStruct((rows, n), F32),
        compiler_params=_cparams(("arbitrary",)),
        name="mod",
    )(c_rows, w_mod, b_mod.reshape(1, n))


def _inproj_kernel(ctx_ref, x_ref, g_ref, mod_ref, w_ref, rw_ref, xb_ref, ggb_ref, sg_ref,
                   *, cols, ntc):
    x = jnp.where(pl.program_id(1) < ntc, ctx_ref[0], x_ref[0])
    xn = x * lax.rsqrt(jnp.mean(x * x, axis=-1, keepdims=True) + RMS_EPS) * g_ref[...]
    sh = mod_ref[0, 0, 0:1, :]
    sc = mod_ref[0, 0, 1:2, :]
    u = (xn * (1.0 + sc) + sh).astype(BF16)
    c_rw, c_xb, c_gb = cols
    d = lambda lo, hi: jnp.dot(u, w_ref[:, lo:hi], preferred_element_type=F32)
    rw_ref[0] = d(0, c_rw)
    xb_ref[0] = d(c_rw, c_xb)
    ggb_ref[0] = jax.nn.gelu(d(c_xb, c_gb))
    sg_ref[0] = _sigmoid(d(c_gb, w_ref.shape[1]))


def _inproj(ctx, x, g1, mod1, w_in_bf, rwkv_cols, lru_w):
    b, tc, d = ctx.shape
    t = tc + x.shape[1]
    n = w_in_bf.shape[1]
    tm = min(ROW_TILE, tc)
    ntc = tc // tm
    cols = (rwkv_cols, rwkv_cols + lru_w, rwkv_cols + 2 * lru_w)
    gate_cols = n - cols[2]
    row_spec = lambda w: pl.BlockSpec((1, tm, w), lambda bi, i: (bi, i, 0))
    return pl.pallas_call(
        functools.partial(_inproj_kernel, cols=cols, ntc=ntc),
        grid=(b, t // tm),
        in_specs=[pl.BlockSpec((1, tm, d), lambda bi, i: (bi, jnp.minimum(i, ntc - 1), 0)),
                  pl.BlockSpec((1, tm, d), lambda bi, i: (bi, jnp.maximum(i - ntc, 0), 0)),
                  pl.BlockSpec((1, d), lambda bi, i: (0, 0)),
                  pl.BlockSpec((1, 1, 2, d), lambda bi, i: (bi, jnp.where(i < ntc, 0, 1), 0, 0)),
                  pl.BlockSpec((d, n), lambda bi, i: (0, 0))],
        out_specs=[row_spec(rwkv_cols), row_spec(lru_w), row_spec(lru_w), row_spec(gate_cols)],
        out_shape=[jax.ShapeDtypeStruct((b, t, rwkv_cols), F32),
                   jax.ShapeDtypeStruct((b, t, lru_w), F32),
                   jax.ShapeDtypeStruct((b, t, lru_w), F32),
                   jax.ShapeDtypeStruct((b, t, gate_cols), F32)],
        compiler_params=_cparams(("parallel", "parallel")),
        name="inproj",
    )(ctx, x, g1, mod1, w_in_bf)


def _shift_down(z, halo_prev, s):
    tm = z.shape[0]
    rolled = pltpu.roll(z, s, 0)
    hp = pltpu.roll(halo_prev, s, 0)
    r8 = lax.broadcasted_iota(jnp.int32, (SUBLANES, 1), 0)
    head = jnp.where(r8 < s, hp, rolled[0:SUBLANES])
    return jnp.concatenate([head, rolled[SUBLANES:tm]], axis=0)


def _shift_up(z, halo_next, s):
    tm = z.shape[0]
    rolled = pltpu.roll(z, tm - s, 0)
    hn = pltpu.roll(halo_next, SUBLANES - s, 0)
    r8 = lax.broadcasted_iota(jnp.int32, (SUBLANES, 1), 0)
    tail = jnp.where(r8 >= SUBLANES - s, hn, rolled[tm - SUBLANES:tm])
    return jnp.concatenate([rolled[0:tm - SUBLANES], tail], axis=0)


def _seg_pos(i, tm, tc, tl):
    grow = lax.broadcasted_iota(jnp.int32, (tm, 1), 0) + i * tm
    in_ctx = grow < tc
    pos = jnp.where(in_ctx, grow, grow - tc)
    seg_len = jnp.where(in_ctx, tc, tl)
    return pos, seg_len


def _halo_specs(tm, t, width):
    nb = tm // SUBLANES
    last = t // SUBLANES - 1
    prev = pl.BlockSpec((1, SUBLANES, width), lambda bi, i: (bi, jnp.maximum(i * nb - 1, 0), 0))
    nxt = pl.BlockSpec((1, SUBLANES, width), lambda bi, i: (bi, jnp.minimum((i + 1) * nb, last), 0))
    return prev, nxt


def _chunk_masks(rev):
    c = CHUNK
    n = 2 * c
    sign = -1 if rev else 1
    ti = lax.broadcasted_iota(jnp.int32, (c, c), 0)
    ii = lax.broadcasted_iota(jnp.int32, (c, c), 1)
    row = lax.broadcasted_iota(jnp.int32, (n, n), 0)
    col = lax.broadcasted_iota(jnp.int32, (n, n), 1)
    same = (row // c) == (col // c)
    delta = (col % c - row % c) * sign
    lane = lax.broadcasted_iota(jnp.int32, (1, n), 1)
    return {
        "cum": jnp.where((ii - ti) * sign <= 0, 1.0, 0.0).astype(BF16),
        "strict": same & (delta < 0),
        "incl": same & (delta <= 0),
        "eye": jnp.where(row == col, 1.0, 0.0).astype(F32),
        "left": lane < c,
        "m0": jnp.where(lane < c, 1.0, 0.0).astype(F32),
        "m1": jnp.where(lane >= c, 1.0, 0.0).astype(F32),
    }


def _chunk_group(units, m, nsq):
    c = CHUNK
    n = 2 * c
    m0, m1 = m["m0"], m["m1"]
    stack = lambda z: jnp.concatenate([z * m0, z * m1], axis=0)

    lcs = [_mm_exact_lhs(m["cum"], u[3]) for u in units]
    pre = []
    for (r, v, kk, lw, kd, bd), lc in zip(units, lcs):
        ltot = jnp.sum(lw, axis=0, keepdims=True)
        en = jnp.exp(-lc)
        eh = jnp.exp(ltot - lc)
        qa_bd = stack(-kk * jnp.exp(lc - lw))
        qr_bd = stack(r * jnp.exp(lc))
        v_bd = stack(v)
        pre.append(dict(q=jnp.concatenate([qa_bd, qr_bd], axis=0), qa=qa_bd, qr=qr_bd,
                        kbx=jnp.concatenate([bd * en, kd * en], axis=0),
                        v0=jnp.concatenate([jnp.zeros_like(v_bd), v_bd], axis=1), v=v_bd,
                        kt=jnp.concatenate([jnp.transpose(stack(bd * eh)),
                                            jnp.transpose(stack(kd * eh))], axis=1),
                        dg=m["eye"] * jnp.exp(ltot)))

    gram = [_mm_nt(p["q"], p["kbx"]) for p in pre]
    swap = [pltpu.roll(g, c, 1) for g in gram]
    gb = [jnp.where(m["left"], g, s) for g, s in zip(gram, swap)]
    gx = [jnp.where(m["left"], s, g) for g, s in zip(gram, swap)]
    gab = [jnp.where(m["strict"], g[:n], 0.0) for g in gb]
    grb = [jnp.where(m["incl"], g[n:], 0.0) for g in gb]
    gak = [jnp.where(m["strict"], g[:n], 0.0) for g in gx]
    grk = [jnp.where(m["incl"], g[n:], 0.0) for g in gx]

    t_inv = [m["eye"] + g for g in gab]
    gp = [_mm(g, g) for g in gab]
    for _ in range(nsq - 1):
        prod = [_mm(jnp.concatenate([t, g], axis=0), g) for t, g in zip(t_inv, gp)]
        t_inv = [t + p[:n] for t, p in zip(t_inv, prod)]
        gp = [p[n:] for p in prod]
    t_inv = [t + _mm(t, g) for t, g in zip(t_inv, gp)]

    gv = [_mm(g, p["v"]) for g, p in zip(gak, pre)]
    tw = [_mm(t, jnp.concatenate([p["qa"], x], axis=1)) for t, p, x in zip(t_inv, pre, gv)]
    big = [_mm(jnp.concatenate([jnp.concatenate([b_, k_], axis=1), p["kt"]], axis=0),
               jnp.concatenate([x, p["v0"]], axis=0))
           for b_, k_, p, x in zip(grb, grk, pre, tw)]

    fold = lambda z: z[0:c] + z[c:2 * c]
    outs = []
    for p, o in zip(pre, big):
        ra = p["qr"] + o[:n, :n]
        yv = o[:n, n:]
        ac = p["dg"] + o[n:, :n]
        sc = o[n:, n:]
        outs.append((fold(ra), fold(yv), fold(ac), fold(sc)))
    return outs


def _front_kernel(rw_ref, prev_ref, next_ref, mu_ref, kkw_ref, ka_ref, w0_ref, a0_ref,
                  w2_ref, a2_ref, g2_ref, e_ref,
                  r_ref, v_ref, g_ref, ks_ref, ra_ref, yv_ref, ac_ref, sc_ref,
                  *, tc, tl, width, nsq):
    i = pl.program_id(1)
    p = rw_ref[0]
    tm = p.shape[0]
    pos, seg_len = _seg_pos(i, tm, tc, tl)
    prev = jnp.where(pos >= 1, _shift_down(p, prev_ref[0], 1), 0.0)
    nxt = jnp.where(pos + 1 < seg_len, _shift_up(p, next_ref[0], 1), 0.0)
    pm = p + mu_ref[...] * (0.5 * (prev + nxt) - p)

    w = width
    r = pm[:, 0:w]
    k = pm[:, w:2 * w]
    v = pm[:, 2 * w:3 * w]
    wa = pm[:, 3 * w:3 * w + RWKV_W_RANK + RWKV_A_RANK]
    gd = pm[:, 3 * w + RWKV_W_RANK + RWKV_A_RANK:]

    kkr = k * kkw_ref[...]
    ss = _mm_split_rhs(kkr * kkr, e_ref[...])
    kk = kkr / jnp.maximum(jnp.sqrt(ss), 1e-12)
    r_ref[0] = r
    v_ref[0] = v
    g_ref[0] = _mm(_sigmoid(gd), g2_ref[...])

    twa = jnp.tanh(wa)
    ka = ka_ref[...]
    per_dir = []
    ksum = jnp.zeros_like(k)
    for d in range(2):
        wpre = w0_ref[d:d + 1, :] + _mm(twa, w2_ref[d])
        lw = -jnp.exp(-_softplus(-wpre) - 0.5)
        eta = _sigmoid(a0_ref[d:d + 1, :] + _mm(wa, a2_ref[d]))
        kd = k * (1.0 + (eta - 1.0) * ka)
        per_dir.append((lw, kd, kk * eta))
        ksum = ksum + kd
    ks_ref[0] = ksum

    n = 2 * CHUNK
    for d, (lw, kd, bd) in enumerate(per_dir):
        m = _chunk_masks(d == 1)
        keys, units = [], []
        for ci in range(tm // CHUNK):
            rows = slice(ci * CHUNK, (ci + 1) * CHUNK)
            for hp in range(w // n):
                lanes = slice(hp * n, (hp + 1) * n)
                keys.append((ci, hp, rows, lanes))
                units.append((r[rows, lanes], v[rows, lanes], kk[rows, lanes],
                              lw[rows, lanes], kd[rows, lanes], bd[rows, lanes]))
        for (ci, hp, rows, lanes), (ra, yv, ac, sc) in zip(keys, _chunk_group(units, m, nsq)):
            ra_ref[d, 0, rows, lanes] = ra
            yv_ref[d, 0, rows, lanes] = yv
            ac_ref[d, 0, ci, hp] = ac
            sc_ref[d, 0, ci, hp] = sc


def _rwkv_front(rw, mu, k_k, k_a, w0, a0, w2p, a2p, g2, e_heads, tc, tl):
    b, t, cols = rw.shape
    w = k_k.shape[-1]
    tm = min(ROW_TILE, tc)
    n = 2 * CHUNK
    nc = t // CHUNK
    cpt = tm // CHUNK
    nsq = (CHUNK - 1).bit_length() - 1
    prev_spec, next_spec = _halo_specs(tm, t, cols)
    full = lambda shape: pl.BlockSpec(shape, lambda bi, i: (0,) * len(shape))
    row = pl.BlockSpec((1, tm, w), lambda bi, i: (bi, i, 0))
    drow = pl.BlockSpec((2, 1, tm, w), lambda bi, i: (0, bi, i, 0))
    mat = pl.BlockSpec((2, 1, cpt, w // n, CHUNK, n), lambda bi, i: (0, bi, i, 0, 0, 0))
    shp = jax.ShapeDtypeStruct((b, t, w), F32)
    dshp = jax.ShapeDtypeStruct((2, b, t, w), F32)
    mshp = jax.ShapeDtypeStruct((2, b, nc, w // n, CHUNK, n), F32)
    return pl.pallas_call(
        functools.partial(_front_kernel, tc=tc, tl=tl, width=w, nsq=nsq),
        grid=(b, t // tm),
        in_specs=[pl.BlockSpec((1, tm, cols), lambda bi, i: (bi, i, 0)), prev_spec, next_spec,
                  full((1, cols)), full((1, w)), full((1, w)), full((2, w)), full((2, w)),
                  full(w2p.shape), full(a2p.shape), full(g2.shape), full(e_heads.shape)],
        out_specs=[row, row, row, row, drow, drow, mat, mat],
        out_shape=[shp, shp, shp, shp, dshp, dshp, mshp, mshp],
        compiler_params=_cparams(("parallel", "parallel")),
        name="front",
    )(rw, rw, rw, mu, k_k, k_a, w0, a0, w2p, a2p, g2, e_heads)


def _carry_kernel(ra0, yv0, ac0, sc0, ra1, yv1, ac1, sc1, y0_ref, y1_ref, st_ref):
    j = pl.program_id(0)

    @pl.when(j == 0)
    def _():
        st_ref[...] = jnp.zeros_like(st_ref)

    nb, c, w = y0_ref.shape
    n = 2 * c
    lane = lax.broadcasted_iota(jnp.int32, (1, n), 1)
    m0 = jnp.where(lane < c, 1.0, 0.0).astype(F32)
    m1 = 1.0 - m0
    stack = lambda z: jnp.concatenate([z * m0, z * m1], axis=0)

    for d, (ra, yv, ac, sc, y_ref) in enumerate(((ra0, yv0, ac0, sc0, y0_ref),
                                                 (ra1, yv1, ac1, sc1, y1_ref))):
        def body(bi, carry, d=d, ra=ra, yv=yv, ac=ac, sc=sc, y_ref=y_ref):
            for hp in range(w // n):
                lanes = slice(hp * n, (hp + 1) * n)
                st = st_ref[d, bi, hp]
                y_ref[bi, :, lanes] = _mm(ra[0, bi, :, lanes], st) + yv[0, bi, :, lanes]
                st_ref[d, bi, hp] = _mm3(stack(ac[0, bi, 0, hp]), st) + stack(sc[0, bi, 0, hp])
            return carry
        lax.fori_loop(0, nb, body, 0)


def _rwkv_carry(ra, yv, ac, sc, tc):
    _, b, t, w = ra.shape
    n = 2 * CHUNK
    nc = t // CHUNK
    ncc = tc // CHUNK
    fwd = lambda j: j
    bwd = lambda j: jnp.where(j < ncc, ncc - 1 - j, nc + ncc - 1 - j)
    row = lambda d, f: pl.BlockSpec((1, b, CHUNK, w), lambda j: (d, 0, f(j), 0))
    mat = lambda d, f: pl.BlockSpec((1, b, 1, w // n, CHUNK, n), lambda j: (d, 0, f(j), 0, 0, 0))
    out = lambda f: pl.BlockSpec((b, CHUNK, w), lambda j: (0, f(j), 0))
    shp = jax.ShapeDtypeStruct((b, t, w), F32)
    return pl.pallas_call(
        _carry_kernel,
        grid=(nc,),
        in_specs=[row(0, fwd), row(0, fwd), mat(0, fwd), mat(0, fwd),
                  row(1, bwd), row(1, bwd), mat(1, bwd), mat(1, bwd)],
        out_specs=[out(fwd), out(bwd)],
        out_shape=[shp, shp],
        scratch_shapes=[pltpu.VMEM((2, b, w // n, n, n), F32)],
        compiler_params=_cparams(("arbitrary",)),
        name="carry",
    )(ra, yv, ac, sc, ra, yv, ac, sc)


def _lru_dir(z, halo, cw_ref, cb_ref, wa_ref, ba_ref, wx_ref, bx_ref, lam_ref, pos, seg_len,
             carry, h_ref, a_scr, b_scr, d):
    rev = d == 1
    tm, w = z.shape
    xc = cb_ref[d:d + 1, :] + cw_ref[d, CONV_W - 1:CONV_W, :] * z
    for s in range(1, CONV_W):
        if rev:
            zs = jnp.where(pos + s < seg_len, _shift_up(z, halo, s), 0.0)
        else:
            zs = jnp.where(pos >= s, _shift_down(z, halo, s), 0.0)
        xc = xc + cw_ref[d, CONV_W - 1 - s:CONV_W - s, :] * zs
    rg = _sigmoid(_mm(xc, wa_ref[d]) + ba_ref[d:d + 1, :])
    ig = _sigmoid(_mm(xc, wx_ref[d]) + bx_ref[d:d + 1, :])
    log_a = -LRU_C * rg * _softplus(-lam_ref[d:d + 1, :])
    a = jnp.exp(log_a)
    q = -jnp.tanh(log_a) * (a * a + 1.0)
    bv = q * lax.rsqrt(jnp.maximum(q, MIN_NORMAL)) * (ig * xc)

    groups = tm // SUBLANES
    nl = w // LANES
    for kl in range(nl):
        a_scr[kl] = a[:, kl * LANES:(kl + 1) * LANES]
        b_scr[kl] = bv[:, kl * LANES:(kl + 1) * LANES]
    rows_of = lambda ref, r: jnp.concatenate(
        [ref[kl, pl.ds(r, groups, stride=SUBLANES), :] for kl in range(nl)], axis=1)
    order = list(reversed(range(SUBLANES))) if rev else list(range(SUBLANES))
    acum, bcum = {}, {}
    pa = pb = None
    for r in order:
        ar, br = rows_of(a_scr, r), rows_of(b_scr, r)
        pa, pb = (ar, br) if pa is None else (ar * pa, br + ar * pb)
        acum[r], bcum[r] = pa, pb
    cin = [None] * groups
    for gi in (reversed(range(groups)) if rev else range(groups)):
        cin[gi] = carry
        carry = pa[gi:gi + 1] * carry + pb[gi:gi + 1]
    cin = jnp.concatenate(cin, axis=0)
    for r in order:
        hr = acum[r] * cin + bcum[r]
        for kl in range(nl):
            a_scr[kl, pl.ds(r, groups, stride=SUBLANES), :] = hr[:, kl * LANES:(kl + 1) * LANES]
    for kl in range(nl):
        h_ref[0, :, kl * LANES:(kl + 1) * LANES] = a_scr[kl]
    return carry


def _lru_kernel(zf_ref, prev_ref, zb_ref, next_ref, cw_ref, cb_ref, wa_ref, ba_ref, wx_ref, bx_ref,
                lam_ref, h0_ref, h1_ref, carry_ref, a_scr, b_scr, *, tc, tl):
    j = pl.program_id(1)

    @pl.when(j == 0)
    def _():
        carry_ref[...] = jnp.zeros_like(carry_ref)

    tm = zf_ref.shape[1]
    nt = (tc + tl) // tm
    ntc = tc // tm
    jb = jnp.where(j < ntc, ntc - 1 - j, nt + ntc - 1 - j)
    params = (cw_ref, cb_ref, wa_ref, ba_ref, wx_ref, bx_ref, lam_ref)
    pos, seg_len = _seg_pos(j, tm, tc, tl)
    carry_ref[0] = _lru_dir(zf_ref[0], prev_ref[0], *params, pos, seg_len, carry_ref[0],
                            h0_ref, a_scr, b_scr, 0)
    pos, seg_len = _seg_pos(jb, tm, tc, tl)
    carry_ref[1] = _lru_dir(zb_ref[0], next_ref[0], *params, pos, seg_len, carry_ref[1],
                            h1_ref, a_scr, b_scr, 1)


def _lru(zs, conv_w, conv_b, wa_bd, ba, wx_bd, bx, lam, tc, tl):
    b, t, w = zs.shape
    tm = min(ROW_TILE, tc)
    nt = t // tm
    ntc = tc // tm
    nb = tm // SUBLANES
    last = t // SUBLANES - 1
    bwd = lambda j: jnp.where(j < ntc, ntc - 1 - j, nt + ntc - 1 - j)
    full = lambda shape: pl.BlockSpec(shape, lambda bi, j: (0,) * len(shape))
    tile_f = pl.BlockSpec((1, tm, w), lambda bi, j: (bi, j, 0))
    tile_b = pl.BlockSpec((1, tm, w), lambda bi, j: (bi, bwd(j), 0))
    prev = pl.BlockSpec((1, SUBLANES, w), lambda bi, j: (bi, jnp.maximum(j * nb - 1, 0), 0))
    nxt = pl.BlockSpec((1, SUBLANES, w), lambda bi, j: (bi, jnp.minimum((bwd(j) + 1) * nb, last), 0))
    shp = jax.ShapeDtypeStruct((b, t, w), F32)
    return pl.pallas_call(
        functools.partial(_lru_kernel, tc=tc, tl=tl),
        grid=(b, nt),
        in_specs=[tile_f, prev, tile_b, nxt,
                  full(conv_w.shape), full(conv_b.shape), full(wa_bd.shape), full(ba.shape),
                  full(wx_bd.shape), full(bx.shape), full(lam.shape)],
        out_specs=[tile_f, tile_b],
        out_shape=[shp, shp],
        scratch_shapes=[pltpu.VMEM((2, 1, w), F32), pltpu.VMEM((w // LANES, tm, LANES), F32),
                        pltpu.VMEM((w // LANES, tm, LANES), F32)],
        compiler_params=_cparams(("parallel", "arbitrary")),
        name="lru",
    )(zs, zs, zs, zs, conv_w, conv_b, wa_bd, ba, wx_bd, bx, lam)


def _merge_kernel(y0_ref, y1_ref, r_ref, v_ref, ks_ref, g_ref, h_ref, ggb_ref,
                  sga_ref, sgb_ref, x_ref, mod_ref, lng_ref, lnb_ref, rk_ref, e_ref,
                  wr_ref, wl_ref, wo_ref, g2_ref, rt_ref,
                  x1_ref, h2_ref, aff_ref, *, n_experts):
    e = e_ref[...]
    inv = 1.0 / HEAD_DIM
    y = y0_ref[0] + y1_ref[0]
    mean = _mm_split_rhs(y, e) * inv
    yc = y - mean
    var = _mm_split_rhs(yc * yc, e) * inv
    yn = yc * lax.rsqrt(var + LNX_EPS) * lng_ref[...] + lnb_ref[...]
    bonus = _mm_split_rhs(r_ref[0] * ks_ref[0] * rk_ref[...], e) * v_ref[0]
    ya = _mm((yn + bonus) * g_ref[0], wr_ref[...])
    yb = _mm(h_ref[0] * ggb_ref[0], wl_ref[...])
    mix = _mm(sga_ref[0] * ya + sgb_ref[0] * yb, wo_ref[...])
    gt1 = mod_ref[0, 0:1, :]
    sh2 = mod_ref[0, 1:2, :]
    sc2 = mod_ref[0, 2:3, :]
    x1 = x_ref[0] + gt1 * mix
    x1_ref[0] = x1
    xn = x1 * lax.rsqrt(jnp.mean(x1 * x1, axis=-1, keepdims=True) + RMS_EPS) * g2_ref[...]
    h2 = xn * (1.0 + sc2) + sh2
    h2_ref[0] = h2.astype(BF16)
    logits = _mm(h2, rt_ref[...])
    col = lax.broadcasted_iota(jnp.int32, logits.shape, 1)
    logits = jnp.where(col < n_experts, logits, -1e30)
    ex = jnp.exp(logits - jnp.max(logits, axis=-1, keepdims=True))
    aff = ex / jnp.sum(ex, axis=-1, keepdims=True)
    aff_ref[0] = jnp.transpose(aff)[0:n_experts, :]


def _merge(y0, y1, r, v, ks, g, h, ggb, sg, x, mod2, lnx_g, lnx_b, r_k, e_heads,
           wr, wl, wo, g2n, router_pad, tc, n_experts):
    b, tl, d = x.shape
    w = r.shape[-1]
    tm = min(ROW_TILE, tc)
    off = tc // tm
    seq = lambda width, blk=0: pl.BlockSpec((1, tm, width), lambda bi, i: (bi, i + off, blk))
    lat = lambda width: pl.BlockSpec((1, tm, width), lambda bi, i: (bi, i, 0))
    full = lambda shape: pl.BlockSpec(shape, lambda bi, i: (0,) * len(shape))
    return pl.pallas_call(
        functools.partial(_merge_kernel, n_experts=n_experts),
        grid=(b, tl // tm),
        in_specs=[seq(w), seq(w), seq(w), seq(w), seq(w), seq(w), lat(w), seq(w),
                  seq(d, 0), seq(d, 1), lat(d),
                  pl.BlockSpec((1, 3, d), lambda bi, i: (bi, 0, 0)),
                  full((1, w)), full((1, w)), full((1, w)), full(e_heads.shape),
                  full(wr.shape), full(wl.shape), full(wo.shape), full((1, d)),
                  full(router_pad.shape)],
        out_specs=[lat(d), lat(d),
                   pl.BlockSpec((1, n_experts, tm), lambda bi, i: (bi, 0, i))],
        out_shape=[jax.ShapeDtypeStruct((b, tl, d), F32),
                   jax.ShapeDtypeStruct((b, tl, d), BF16),
                   jax.ShapeDtypeStruct((b, n_experts, tl), F32)],
        compiler_params=_cparams(("parallel", "parallel")),
        name="merge",
    )(y0, y1, r, v, ks, g, h, ggb, sg, sg, x, mod2, lnx_g, lnx_b, r_k, e_heads,
      wr, wl, wo, g2n, router_pad)


def _prefix_incl(x, lane):
    n = x.shape[-1]
    s = 1
    while s < n:
        x = x + jnp.where(lane >= s, pltpu.roll(x, s, 1), 0.0)
        s *= 2
    return x


def _route_kernel(aff_ref, slot_ref, *, cap):
    aff = aff_ref[0]
    ne, t = aff.shape
    count_ge = lambda v: jnp.sum(jnp.where(aff >= v, 1.0, 0.0), axis=-1, keepdims=True)

    def bit_step(i, thr):
        cand = thr | jnp.left_shift(jnp.int32(1), 30 - i)
        return jnp.where(count_ge(pltpu.bitcast(cand, F32)) >= cap, cand, thr)

    thr = lax.fori_loop(0, 31, bit_step, jnp.zeros((ne, 1), jnp.int32))
    lo = pltpu.bitcast(thr, F32)
    hi = pltpu.bitcast(jnp.maximum(thr + 1, MIN_NORMAL_BITS), F32)

    def halve(_, lh):
        lo, hi = lh
        mid = lo + 0.5 * (hi - lo)
        ge = count_ge(mid) >= cap
        return jnp.where(ge, mid, lo), jnp.where(ge, hi, mid)

    lo, hi = lax.fori_loop(0, 30, halve, (lo, hi))
    gt = aff >= hi
    eq = jnp.where((aff >= lo) & (aff < hi), 1.0, 0.0)
    need = cap - jnp.sum(jnp.where(gt, 1.0, 0.0), axis=-1, keepdims=True)
    lane = lax.broadcasted_iota(jnp.int32, (ne, t), 1)
    eq_rank = _prefix_incl(eq, lane) - eq
    sel = jnp.where(gt, 1.0, jnp.where(eq_rank < need, eq, 0.0))
    slot = _prefix_incl(sel, lane) - sel
    slot_ref[0] = jnp.where(sel > 0.0, slot, -1.0)


def _route(aff, cap):
    b, ne, t = aff.shape
    spec = pl.BlockSpec((1, ne, t), lambda bi: (bi, 0, 0))
    return pl.pallas_call(
        functools.partial(_route_kernel, cap=cap),
        grid=(b,),
        in_specs=[spec],
        out_specs=spec,
        out_shape=jax.ShapeDtypeStruct((b, ne, t), F32),
        compiler_params=_cparams(("parallel",)),
        name="route",
    )(aff)


def _ffn_kernel(slot_ref, aff_ref, h2_ref, w1_ref, w3_ref, w2_ref, y_ref, *, cap):
    slots = lax.broadcasted_iota(jnp.int32, (cap, 1), 0).astype(F32)
    hit = slot_ref[0, 0] == slots
    gate = jnp.sum(jnp.where(hit, aff_ref[0, 0], 0.0), axis=-1, keepdims=True)
    onehot = jnp.where(hit, 1.0, 0.0).astype(BF16)
    xs = jnp.dot(onehot, h2_ref[0], preferred_element_type=F32).astype(BF16)
    a = jnp.dot(xs, w1_ref[0], preferred_element_type=F32)
    hid = (a * _sigmoid(a)) * jnp.dot(xs, w3_ref[0], preferred_element_type=F32)
    y = jnp.dot(hid.astype(BF16), w2_ref[0], preferred_element_type=F32) * gate
    y_ref[0, 0] = y.astype(BF16)


def _ffn(slot4, aff4, h2, w1, w3, w2, cap):
    b, ne, _, t = slot4.shape
    d = h2.shape[-1]
    f = w1.shape[-1]
    row = pl.BlockSpec((1, 1, 1, t), lambda bi, e: (bi, e, 0, 0))
    return pl.pallas_call(
        functools.partial(_ffn_kernel, cap=cap),
        grid=(b, ne),
        in_specs=[row, row,
                  pl.BlockSpec((1, t, d), lambda bi, e: (bi, 0, 0)),
                  pl.BlockSpec((1, d, f), lambda bi, e: (e, 0, 0)),
                  pl.BlockSpec((1, d, f), lambda bi, e: (e, 0, 0)),
                  pl.BlockSpec((1, f, d), lambda bi, e: (e, 0, 0))],
        out_specs=pl.BlockSpec((1, 1, cap, d), lambda bi, e: (bi, e, 0, 0)),
        out_shape=jax.ShapeDtypeStruct((b, ne, cap, d), BF16),
        compiler_params=_cparams(("parallel", "arbitrary")),
        name="ffn",
    )(slot4, aff4, h2, w1, w3, w2)


def _combine_kernel(slot_ref, y_ref, x1_ref, gt_ref, g_ref, o_ref, acc_ref, *, cap):
    e = pl.program_id(2)

    @pl.when(e == 0)
    def _():
        acc_ref[...] = jnp.zeros_like(acc_ref)

    slots = lax.broadcasted_iota(jnp.int32, (cap, 1), 0).astype(F32)
    onehot = jnp.where(slot_ref[0, 0] == slots, 1.0, 0.0).astype(BF16)
    acc_ref[...] += lax.dot_general(onehot, y_ref[0, 0], (((0,), (0,)), ((), ())),
                                    preferred_element_type=F32)

    @pl.when(e == pl.num_programs(2) - 1)
    def _():
        x2 = x1_ref[0] + gt_ref[0] * acc_ref[...]
        o_ref[0] = x2 * lax.rsqrt(jnp.mean(x2 * x2, axis=-1, keepdims=True) + RMS_EPS) * g_ref[...]


def _combine(slot4, ye, x1, gt2, final_g, cap):
    b, tl, d = x1.shape
    ne = slot4.shape[1]
    tt = min(4 * ROW_TILE, tl)
    row = pl.BlockSpec((1, tt, d), lambda bi, j, e: (bi, j, 0))
    return pl.pallas_call(
        functools.partial(_combine_kernel, cap=cap),
        grid=(b, tl // tt, ne),
        in_specs=[pl.BlockSpec((1, 1, 1, tt), lambda bi, j, e: (bi, e, 0, j)),
                  pl.BlockSpec((1, 1, cap, d), lambda bi, j, e: (bi, e, 0, 0)),
                  row,
                  pl.BlockSpec((1, 1, d), lambda bi, j, e: (bi, 0, 0)),
                  pl.BlockSpec((1, d), lambda bi, j, e: (0, 0))],
        out_specs=row,
        out_shape=jax.ShapeDtypeStruct((b, tl, d), F32),
        scratch_shapes=[pltpu.VMEM((tt, d), F32)],
        compiler_params=_cparams(("parallel", "parallel", "arbitrary")),
        name="combine",
    )(slot4, ye, x1, gt2, final_g)


def _block_diag(w):
    nb, n, _ = w.shape
    eye = jnp.eye(nb, dtype=w.dtype)
    return jnp.einsum("hij,hg->higj", w, eye).reshape(nb * n, nb * n)


def kernel(x, c, ctx, c_ctx, w_mod, b_mod, norm1_g, norm2_g, w_in, mu_rwkv, rwkv_w0, rwkv_w2, rwkv_a0, rwkv_a2, rwkv_g2, rwkv_k_k, rwkv_k_a, rwkv_r_k, lnx_g, lnx_b, w_proj_rwkv, conv_w, conv_b, lru_wa, lru_ba, lru_wx, lru_bx, lru_lam, w_proj_lru, w_out, router, moe_w1, moe_w3, moe_w2, final_g):
    depth = w_mod.shape[0]
    assert depth == 1, "single-layer configuration"
    b, tl, d = x.shape
    tc = ctx.shape[1]
    w = rwkv_k_k.shape[-1]
    lw_ = lru_lam.shape[-1]
    rwkv_cols = mu_rwkv.shape[-1]
    n_experts = router.shape[-1]
    cap = EC_CAPACITY * tl // n_experts
    tm = min(ROW_TILE, tc)
    assert tc % tm == 0 and tl % tm == 0 and tm % CHUNK == 0
    layer = 0

    pad_rows = -(b + 1) % SUBLANES
    c_rows = jnp.concatenate([c, c_ctx[None], jnp.zeros((pad_rows, d), F32)], axis=0)
    m = _modulation(c_rows, w_mod[layer], b_mod[layer])
    m_lat = m[:b].reshape(b, 6, d)
    m_ctx = jnp.broadcast_to(m[b].reshape(1, 6, d), (b, 6, d))
    mod1 = jnp.stack([m_ctx[:, 0:2], m_lat[:, 0:2]], axis=1)
    mod2 = m_lat[:, 2:5]
    gt2 = m_lat[:, 5:6]

    rw, xb, ggb, sg = _inproj(ctx, x, norm1_g[layer][None], mod1, w_in[layer].astype(BF16),
                              rwkv_cols, lw_)

    heads = w // HEAD_DIM
    e_heads = _block_diag(jnp.ones((heads, HEAD_DIM, HEAD_DIM), BF16))
    zpad = jnp.zeros((2, RWKV_W_RANK, w), F32)
    w2p = jnp.concatenate([rwkv_w2[layer], zpad], axis=1)
    a2p = jnp.concatenate([zpad, rwkv_a2[layer]], axis=1)
    r, v, g, ks, ra, yv, ac, sc = _rwkv_front(
        rw, mu_rwkv[layer][None], rwkv_k_k[layer][None], rwkv_k_a[layer][None],
        rwkv_w0[layer], rwkv_a0[layer], w2p, a2p, rwkv_g2[layer], e_heads, tc, tl)
    y0, y1 = _rwkv_carry(ra, yv, ac, sc, tc)

    rows = tl // GRID_W
    xb_lat = xb[:, tc:].reshape(b, rows, GRID_W, lw_).transpose(0, 2, 1, 3).reshape(b, tl, lw_)
    zs = jnp.concatenate([xb[:, :tc], xb_lat], axis=1)
    h0, h1 = _lru(zs, conv_w[layer], conv_b[layer],
                  jax.vmap(_block_diag)(lru_wa[layer]), lru_ba[layer],
                  jax.vmap(_block_diag)(lru_wx[layer]), lru_bx[layer], lru_lam[layer], tc, tl)
    h_lat = (h0[:, tc:] + h1[:, tc:]).reshape(b, GRID_W, rows, lw_).transpose(0, 2, 1, 3).reshape(b, tl, lw_)

    router_pad = jnp.pad(router[layer], ((0, 0), (0, LANES - n_experts)))
    x1, h2, aff = _merge(y0, y1, r, v, ks, g, h_lat, ggb, sg, x, mod2,
                         lnx_g[layer][None], lnx_b[layer][None], rwkv_r_k[layer].reshape(1, w),
                         e_heads, w_proj_rwkv[layer].astype(BF16), w_proj_lru[layer].astype(BF16),
                         w_out[layer].astype(BF16), norm2_g[layer][None], router_pad, tc, n_experts)

    slot4 = _route(aff, cap).reshape(b, n_experts, 1, tl)
    ye = _ffn(slot4, aff.reshape(b, n_experts, 1, tl), h2, moe_w1[layer].astype(BF16),
              moe_w3[layer].astype(BF16), moe_w2[layer].astype(BF16), cap)
    return _combine(slot4, ye, x1, gt2, final_g[None], cap)
```

```python
import functools

import jax
import jax.numpy as jnp
from jax import lax
from jax.experimental import pallas as pl
from jax.experimental.pallas import tpu as pltpu

F32 = jnp.float32
BF16 = jnp.bfloat16

GRID_W = 64
HEAD_DIM = 64
RWKV_W_RANK = 64
RWKV_A_RANK = 64
RWKV_G_RANK = 128
CONV_W = 4
LRU_C = 8.0
EC_CAPACITY = 2
RMS_EPS = 1e-6
LNX_EPS = 64e-5
MIN_NORMAL_BITS = 0x00800000
MIN_NORMAL = 1.1754944e-38

LANES = 128
SUBLANES = 8
CHUNK = 64
ROW_TILE = 256
VMEM_LIMIT = 56 * 1024 * 1024


def _cparams(sem):
    return pltpu.CompilerParams(dimension_semantics=sem, vmem_limit_bytes=VMEM_LIMIT)


def _mm(a, b):
    return jnp.dot(a.astype(BF16), b.astype(BF16), preferred_element_type=F32)


def _split3(a):
    hi = a.astype(BF16)
    r1 = a - hi.astype(F32)
    mid = r1.astype(BF16)
    lo = (r1 - mid.astype(F32)).astype(BF16)
    return hi, mid, lo


def _mm_split_rhs(a, b01):
    hi = a.astype(BF16)
    lo = (a - hi.astype(F32)).astype(BF16)
    d = lambda z: jnp.dot(z, b01, preferred_element_type=F32)
    return d(hi) + d(lo)


def _mm_exact_lhs(a01, b):
    hi, mid, lo = _split3(b)
    d = lambda z: jnp.dot(a01, z, preferred_element_type=F32)
    return d(hi) + d(mid) + d(lo)


def _softplus(z):
    return jnp.maximum(z, 0.0) + jnp.log(1.0 + jnp.exp(-jnp.abs(z)))


def _sigmoid(z):
    return 1.0 / (1.0 + jnp.exp(-z))


def _mod_kernel(c_ref, w_ref, b_ref, o_ref):
    c = c_ref[...]
    s = c * _sigmoid(c)
    o_ref[...] = jnp.dot(s, w_ref[...], precision=lax.Precision.HIGHEST,
                         preferred_element_type=F32) + b_ref[...]


def _modulation(c_rows, w_mod, b_mod):
    rows, d = c_rows.shape
    n = w_mod.shape[1]
    tn = 1536
    return pl.pallas_call(
        _mod_kernel,
        grid=(n // tn,),
        in_specs=[pl.BlockSpec((rows, d), lambda j: (0, 0)),
                  pl.BlockSpec((d, tn), lambda j: (0, j)),
                  pl.BlockSpec((1, tn), lambda j: (0, j))],
        out_specs=pl.BlockSpec((rows, tn), lambda j: (0, j)),
        out_shape=jax.ShapeDtypeStruct((rows, n), F32),
        compiler_params=_cparams(("arbitrary",)),
        name="mod",
    )(c_rows, w_mod, b_mod.reshape(1, n))


def _inproj_kernel(ctx_ref, x_ref, g_ref, mod_ref, w_ref, rw_ref, xb_ref, ggb_ref, sg_ref,
                   *, cols, ntc):
    x = jnp.where(pl.program_id(1) < ntc, ctx_ref[0], x_ref[0])
    xn = x * lax.rsqrt(jnp.mean(x * x, axis=-1, keepdims=True) + RMS_EPS) * g_ref[...]
    sh = mod_ref[0, 0, 0:1, :]
    sc = mod_ref[0, 0, 1:2, :]
    u = (xn * (1.0 + sc) + sh).astype(BF16)
    c_rw, c_xb, c_gb = cols
    d = lambda lo, hi: jnp.dot(u, w_ref[:, lo:hi], preferred_element_type=F32)
    rw_ref[0] = d(0, c_rw)
    xb_ref[0] = d(c_rw, c_xb)
    ggb_ref[0] = jax.nn.gelu(d(c_xb, c_gb))
    sg_ref[0] = _sigmoid(d(c_gb, w_ref.shape[1]))


def _inproj(ctx, x, g1, mod1, w_in_bf, rwkv_cols, lru_w):
    b, tc, d = ctx.shape
    t = tc + x.shape[1]
    n = w_in_bf.shape[1]
    tm = min(ROW_TILE, tc)
    ntc = tc // tm
    cols = (rwkv_cols, rwkv_cols + lru_w, rwkv_cols + 2 * lru_w)
    gate_cols = n - cols[2]
    row_spec = lambda w: pl.BlockSpec((1, tm, w), lambda bi, i: (bi, i, 0))
    return pl.pallas_call(
        functools.partial(_inproj_kernel, cols=cols, ntc=ntc),
        grid=(b, t // tm),
        in_specs=[pl.BlockSpec((1, tm, d), lambda bi, i: (bi, jnp.minimum(i, ntc - 1), 0)),
                  pl.BlockSpec((1, tm, d), lambda bi, i: (bi, jnp.maximum(i - ntc, 0), 0)),
                  pl.BlockSpec((1, d), lambda bi, i: (0, 0)),
                  pl.BlockSpec((1, 1, 2, d), lambda bi, i: (bi, jnp.where(i < ntc, 0, 1), 0, 0)),
                  pl.BlockSpec((d, n), lambda bi, i: (0, 0))],
        out_specs=[row_spec(rwkv_cols), row_spec(lru_w), row_spec(lru_w), row_spec(gate_cols)],
        out_shape=[jax.ShapeDtypeStruct((b, t, rwkv_cols), F32),
                   jax.ShapeDtypeStruct((b, t, lru_w), F32),
                   jax.ShapeDtypeStruct((b, t, lru_w), F32),
                   jax.ShapeDtypeStruct((b, t, gate_cols), F32)],
        compiler_params=_cparams(("parallel", "parallel")),
        name="inproj",
    )(ctx, x, g1, mod1, w_in_bf)


def _shift_down(z, halo_prev, s):
    tm = z.shape[0]
    rolled = pltpu.roll(z, s, 0)
    hp = pltpu.roll(halo_prev, s, 0)
    r8 = lax.broadcasted_iota(jnp.int32, (SUBLANES, 1), 0)
    head = jnp.where(r8 < s, hp, rolled[0:SUBLANES])
    return jnp.concatenate([head, rolled[SUBLANES:tm]], axis=0)


def _shift_up(z, halo_next, s):
    tm = z.shape[0]
    rolled = pltpu.roll(z, tm - s, 0)
    hn = pltpu.roll(halo_next, SUBLANES - s, 0)
    r8 = lax.broadcasted_iota(jnp.int32, (SUBLANES, 1), 0)
    tail = jnp.where(r8 >= SUBLANES - s, hn, rolled[tm - SUBLANES:tm])
    return jnp.concatenate([rolled[0:tm - SUBLANES], tail], axis=0)


def _seg_pos(i, tm, tc, tl):
    grow = lax.broadcasted_iota(jnp.int32, (tm, 1), 0) + i * tm
    in_ctx = grow < tc
    pos = jnp.where(in_ctx, grow, grow - tc)
    seg_len = jnp.where(in_ctx, tc, tl)
    return pos, seg_len


def _halo_specs(tm, t, width):
    nb = tm // SUBLANES
    last = t // SUBLANES - 1
    prev = pl.BlockSpec((1, SUBLANES, width), lambda bi, i: (bi, jnp.maximum(i * nb - 1, 0), 0))
    nxt = pl.BlockSpec((1, SUBLANES, width), lambda bi, i: (bi, jnp.minimum((i + 1) * nb, last), 0))
    return prev, nxt


def _chunk_masks(rev):
    c = CHUNK
    n = 2 * c
    sign = -1 if rev else 1
    ti = lax.broadcasted_iota(jnp.int32, (c, c), 0)
    ii = lax.broadcasted_iota(jnp.int32, (c, c), 1)
    row = lax.broadcasted_iota(jnp.int32, (c, n), 0)
    idx = lax.broadcasted_iota(jnp.int32, (c, n), 1) % c
    delta = (idx - row) * sign
    lane = lax.broadcasted_iota(jnp.int32, (1, n), 1)
    return {
        "cum": jnp.where((ii - ti) * sign <= 0, 1.0, 0.0).astype(BF16),
        "strict": delta < 0,
        "incl": delta <= 0,
        "eye": jnp.where(idx == row, 1.0, 0.0).astype(F32),
        "m0": jnp.where(lane < c, 1.0, 0.0).astype(BF16),
        "m1": jnp.where(lane >= c, 1.0, 0.0).astype(BF16),
    }


def _chunk_group(units, m, nsq):
    c = CHUNK
    n = 2 * c
    bf = lambda z: z.astype(BF16)
    dot = lambda x, y: jnp.dot(x, y, preferred_element_type=F32)
    m0, m1 = m["m0"], m["m1"]
    blockdiag = lambda zb: jnp.concatenate([zb * m0, zb * m1], axis=0)

    def head_t(z):
        zt = jnp.transpose(z)
        return jnp.concatenate([zt[0:c], zt[c:n]], axis=1)

    lcs = [_mm_exact_lhs(m["cum"], u[3]) for u in units]
    pre = []
    for (r, v, kk, lw, kd, bd), lc in zip(units, lcs):
        ltot = jnp.sum(lw, axis=0, keepdims=True)
        en = jnp.exp(-lc)
        eh = jnp.exp(ltot - lc)
        qr = r * jnp.exp(lc)
        qa = bf(-kk * jnp.exp(lc - lw))
        v_bd = blockdiag(bf(v))
        pre.append(dict(q=jnp.concatenate([qa, bf(qr)], axis=0), qa_bd=blockdiag(qa), qr=qr,
                        kbx=jnp.concatenate([blockdiag(bf(bd * en)), blockdiag(bf(kd * en))], axis=0),
                        v_bd=v_bd, v0=jnp.concatenate([jnp.zeros_like(v_bd), v_bd], axis=1),
                        kt=bf(jnp.concatenate([head_t(bd * eh), head_t(kd * eh)], axis=1)),
                        dg=m["eye"] * jnp.exp(ltot)))

    gram = [lax.dot_general(p["q"], p["kbx"], (((1,), (1,)), ((), ())), preferred_element_type=F32)
            for p in pre]
    gab_f = [jnp.where(m["strict"], g[:c, :n], 0.0) for g in gram]
    gak = [bf(jnp.where(m["strict"], g[:c, n:], 0.0)) for g in gram]
    grbk = [bf(jnp.concatenate([jnp.where(m["incl"], g[c:, :n], 0.0),
                                jnp.where(m["incl"], g[c:, n:], 0.0)], axis=1)) for g in gram]

    t_inv = [m["eye"] + g for g in gab_f]
    gp = [bf(g) for g in gab_f]
    gp = [bf(dot(g, blockdiag(g))) for g in gp]
    for _ in range(nsq - 1):
        prod = [dot(jnp.concatenate([bf(t), g], axis=0), blockdiag(g)) for t, g in zip(t_inv, gp)]
        t_inv = [t + p[:c] for t, p in zip(t_inv, prod)]
        gp = [bf(p[c:]) for p in prod]
    t_inv = [bf(t + dot(bf(t), blockdiag(g))) for t, g in zip(t_inv, gp)]

    gv = [bf(dot(g, p["v_bd"])) for g, p in zip(gak, pre)]
    tw = [bf(dot(t, jnp.concatenate([p["qa_bd"], blockdiag(x)], axis=1)))
          for t, p, x in zip(t_inv, pre, gv)]
    big = [dot(jnp.concatenate([g, p["kt"]], axis=0),
               jnp.concatenate([jnp.concatenate([blockdiag(x[:, :n]), blockdiag(x[:, n:])], axis=1),
                                p["v0"]], axis=0))
           for g, p, x in zip(grbk, pre, tw)]
    return [(p["qr"] + o[:c, :n], o[:c, n:], p["dg"] + o[c:, :n], o[c:, n:])
            for p, o in zip(pre, big)]


def _front_kernel(rw_ref, prev_ref, next_ref, mu_ref, kkw_ref, ka_ref, w0_ref, a0_ref,
                  w2_ref, a2_ref, g2_ref, e_ref,
                  r_ref, v_ref, g_ref, ks_ref, ra_ref, yv_ref, ac_ref, sc_ref,
                  *, tc, tl, width, nsq):
    i = pl.program_id(1)
    p = rw_ref[0]
    tm = p.shape[0]
    pos, seg_len = _seg_pos(i, tm, tc, tl)
    prev = jnp.where(pos >= 1, _shift_down(p, prev_ref[0], 1), 0.0)
    nxt = jnp.where(pos + 1 < seg_len, _shift_up(p, next_ref[0], 1), 0.0)
    pm = p + mu_ref[...] * (0.5 * (prev + nxt) - p)

    w = width
    r = pm[:, 0:w]
    k = pm[:, w:2 * w]
    v = pm[:, 2 * w:3 * w]
    wa = pm[:, 3 * w:3 * w + RWKV_W_RANK + RWKV_A_RANK]
    gd = pm[:, 3 * w + RWKV_W_RANK + RWKV_A_RANK:]

    kkr = k * kkw_ref[...]
    ss = _mm_split_rhs(kkr * kkr, e_ref[...])
    kk = kkr / jnp.maximum(jnp.sqrt(ss), 1e-12)
    r_ref[0] = r
    v_ref[0] = v
    g_ref[0] = _mm(_sigmoid(gd), g2_ref[...])

    twa = jnp.tanh(wa)
    ka = ka_ref[...]
    per_dir = []
    ksum = jnp.zeros_like(k)
    for d in range(2):
        wpre = w0_ref[d:d + 1, :] + _mm(twa, w2_ref[d])
        lw = -jnp.exp(-_softplus(-wpre) - 0.5)
        eta = _sigmoid(a0_ref[d:d + 1, :] + _mm(wa, a2_ref[d]))
        kd = k * (1.0 + (eta - 1.0) * ka)
        per_dir.append((lw, kd, kk * eta))
        ksum = ksum + kd
    ks_ref[0] = ksum

    n = 2 * CHUNK
    for d, (lw, kd, bd) in enumerate(per_dir):
        m = _chunk_masks(d == 1)
        keys, units = [], []
        for ci in range(tm // CHUNK):
            rows = slice(ci * CHUNK, (ci + 1) * CHUNK)
            for hp in range(w // n):
                lanes = slice(hp * n, (hp + 1) * n)
                keys.append((ci, hp, rows, lanes))
                units.append((r[rows, lanes], v[rows, lanes], kk[rows, lanes],
                              lw[rows, lanes], kd[rows, lanes], bd[rows, lanes]))
        for (ci, hp, rows, lanes), (ra, yv, ac, sc) in zip(keys, _chunk_group(units, m, nsq)):
            ra_ref[d, 0, rows, lanes] = ra
            yv_ref[d, 0, rows, lanes] = yv
            ac_ref[d, 0, ci, hp] = ac
            sc_ref[d, 0, ci, hp] = sc


def _rwkv_front(rw, mu, k_k, k_a, w0, a0, w2p, a2p, g2, e_heads, tc, tl):
    b, t, cols = rw.shape
    w = k_k.shape[-1]
    tm = min(ROW_TILE, tc)
    n = 2 * CHUNK
    nc = t // CHUNK
    cpt = tm // CHUNK
    nsq = (CHUNK - 1).bit_length() - 1
    prev_spec, next_spec = _halo_specs(tm, t, cols)
    full = lambda shape: pl.BlockSpec(shape, lambda bi, i: (0,) * len(shape))
    row = pl.BlockSpec((1, tm, w), lambda bi, i: (bi, i, 0))
    drow = pl.BlockSpec((2, 1, tm, w), lambda bi, i: (0, bi, i, 0))
    mat = pl.BlockSpec((2, 1, cpt, w // n, CHUNK, n), lambda bi, i: (0, bi, i, 0, 0, 0))
    shp = jax.ShapeDtypeStruct((b, t, w), F32)
    dshp = jax.ShapeDtypeStruct((2, b, t, w), F32)
    mshp = jax.ShapeDtypeStruct((2, b, nc, w // n, CHUNK, n), F32)
    return pl.pallas_call(
        functools.partial(_front_kernel, tc=tc, tl=tl, width=w, nsq=nsq),
        grid=(b, t // tm),
        in_specs=[pl.BlockSpec((1, tm, cols), lambda bi, i: (bi, i, 0)), prev_spec, next_spec,
                  full((1, cols)), full((1, w)), full((1, w)), full((2, w)), full((2, w)),
                  full(w2p.shape), full(a2p.shape), full(g2.shape), full(e_heads.shape)],
        out_specs=[row, row, row, row, drow, drow, mat, mat],
        out_shape=[shp, shp, shp, shp, dshp, dshp, mshp, mshp],
        compiler_params=_cparams(("parallel", "parallel")),
        name="front",
    )(rw, rw, rw, mu, k_k, k_a, w0, a0, w2p, a2p, g2, e_heads)


def _carry_kernel(ra0, yv0, ac0, sc0, ra1, yv1, ac1, sc1, y0_ref, y1_ref, st_ref):
    j = pl.program_id(0)

    @pl.when(j == 0)
    def _():
        st_ref[...] = jnp.zeros_like(st_ref)

    nb, c, w = y0_ref.shape
    n = 2 * c
    lane = lax.broadcasted_iota(jnp.int32, (1, n), 1)
    m0 = jnp.where(lane < c, 1.0, 0.0).astype(BF16)
    m1 = jnp.where(lane >= c, 1.0, 0.0).astype(BF16)
    blockdiag = lambda zb: jnp.concatenate([zb * m0, zb * m1], axis=0)
    dot = lambda x, y: jnp.dot(x, y, preferred_element_type=F32)

    srcs = ((ra0, yv0, ac0, sc0, y0_ref), (ra1, yv1, ac1, sc1, y1_ref))
    units = [(d, bi, hp) for d in range(2) for bi in range(nb) for hp in range(w // n)]
    lanes = lambda hp: slice(hp * n, (hp + 1) * n)
    sts = [st_ref[d, bi, hp] for d, bi, hp in units]
    st_hi = [s.astype(BF16) for s in sts]
    st_lo = [(s - h.astype(F32)).astype(BF16) for s, h in zip(sts, st_hi)]
    bd_hi = [blockdiag(h) for h in st_hi]
    bd_lo = [blockdiag(l) for l in st_lo]
    ys = [dot(srcs[d][0][0, bi, :, lanes(hp)].astype(BF16), h) for (d, bi, hp), h in zip(units, bd_hi)]
    for (d, bi, hp), y in zip(units, ys):
        srcs[d][4][bi, :, lanes(hp)] = y + srcs[d][1][0, bi, :, lanes(hp)]
    acs = [srcs[d][2][0, bi, 0, hp] for d, bi, hp in units]
    ac_hi = [a.astype(BF16) for a in acs]
    ac_lo = [(a - h.astype(F32)).astype(BF16) for a, h in zip(acs, ac_hi)]
    news = [dot(ah, bh) + dot(al, bh) + dot(ah, bl)
            for ah, al, bh, bl in zip(ac_hi, ac_lo, bd_hi, bd_lo)]
    for (d, bi, hp), new in zip(units, news):
        st_ref[d, bi, hp] = new + srcs[d][3][0, bi, 0, hp]


def _rwkv_carry(ra, yv, ac, sc, tc):
    _, b, t, w = ra.shape
    n = 2 * CHUNK
    nc = t // CHUNK
    ncc = tc // CHUNK
    fwd = lambda j: j
    bwd = lambda j: jnp.where(j < ncc, ncc - 1 - j, nc + ncc - 1 - j)
    row = lambda d, f: pl.BlockSpec((1, b, CHUNK, w), lambda j: (d, 0, f(j), 0))
    mat = lambda d, f: pl.BlockSpec((1, b, 1, w // n, CHUNK, n), lambda j: (d, 0, f(j), 0, 0, 0))
    out = lambda f: pl.BlockSpec((b, CHUNK, w), lambda j: (0, f(j), 0))
    shp = jax.ShapeDtypeStruct((b, t, w), F32)
    return pl.pallas_call(
        _carry_kernel,
        grid=(nc,),
        in_specs=[row(0, fwd), row(0, fwd), mat(0, fwd), mat(0, fwd),
                  row(1, bwd), row(1, bwd), mat(1, bwd), mat(1, bwd)],
        out_specs=[out(fwd), out(bwd)],
        out_shape=[shp, shp],
        scratch_shapes=[pltpu.VMEM((2, b, w // n, CHUNK, n), F32)],
        compiler_params=_cparams(("arbitrary",)),
        name="carry",
    )(ra, yv, ac, sc, ra, yv, ac, sc)


def _lru_dir(z, halo, cw_ref, cb_ref, wa_ref, ba_ref, wx_ref, bx_ref, lam_ref, pos, seg_len,
             carry, h_ref, a_scr, b_scr, d):
    rev = d == 1
    tm, w = z.shape
    xc = cb_ref[d:d + 1, :] + cw_ref[d, CONV_W - 1:CONV_W, :] * z
    for s in range(1, CONV_W):
        if rev:
            zs = jnp.where(pos + s < seg_len, _shift_up(z, halo, s), 0.0)
        else:
            zs = jnp.where(pos >= s, _shift_down(z, halo, s), 0.0)
        xc = xc + cw_ref[d, CONV_W - 1 - s:CONV_W - s, :] * zs
    rg = _sigmoid(_mm(xc, wa_ref[d]) + ba_ref[d:d + 1, :])
    ig = _sigmoid(_mm(xc, wx_ref[d]) + bx_ref[d:d + 1, :])
    log_a = -LRU_C * rg * _softplus(-lam_ref[d:d + 1, :])
    a = jnp.exp(log_a)
    q = -jnp.tanh(log_a) * (a * a + 1.0)
    bv = q * lax.rsqrt(jnp.maximum(q, MIN_NORMAL)) * (ig * xc)

    groups = tm // SUBLANES
    nl = w // LANES
    for kl in range(nl):
        a_scr[kl] = a[:, kl * LANES:(kl + 1) * LANES]
        b_scr[kl] = bv[:, kl * LANES:(kl + 1) * LANES]
    rows_of = lambda ref, r: jnp.concatenate(
        [ref[kl, pl.ds(r, groups, stride=SUBLANES), :] for kl in range(nl)], axis=1)
    order = list(reversed(range(SUBLANES))) if rev else list(range(SUBLANES))
    acum, bcum = {}, {}
    pa = pb = None
    for r in order:
        ar, br = rows_of(a_scr, r), rows_of(b_scr, r)
        pa, pb = (ar, br) if pa is None else (ar * pa, br + ar * pb)
        acum[r], bcum[r] = pa, pb
    cin = [None] * groups
    for gi in (reversed(range(groups)) if rev else range(groups)):
        cin[gi] = carry
        carry = pa[gi:gi + 1] * carry + pb[gi:gi + 1]
    cin = jnp.concatenate(cin, axis=0)
    for r in order:
        hr = acum[r] * cin + bcum[r]
        for kl in range(nl):
            a_scr[kl, pl.ds(r, groups, stride=SUBLANES), :] = hr[:, kl * LANES:(kl + 1) * LANES]
    for kl in range(nl):
        h_ref[0, :, kl * LANES:(kl + 1) * LANES] = a_scr[kl]
    return carry


def _lru_kernel(zf_ref, prev_ref, zb_ref, next_ref, cw_ref, cb_ref, wa_ref, ba_ref, wx_ref, bx_ref,
                lam_ref, h0_ref, h1_ref, carry_ref, a_scr, b_scr, *, tc, tl):
    j = pl.program_id(1)

    @pl.when(j == 0)
    def _():
        carry_ref[...] = jnp.zeros_like(carry_ref)

    tm = zf_ref.shape[1]
    nt = (tc + tl) // tm
    ntc = tc // tm
    jb = jnp.where(j < ntc, ntc - 1 - j, nt + ntc - 1 - j)
    params = (cw_ref, cb_ref, wa_ref, ba_ref, wx_ref, bx_ref, lam_ref)
    pos, seg_len = _seg_pos(j, tm, tc, tl)
    carry_ref[0] = _lru_dir(zf_ref[0], prev_ref[0], *params, pos, seg_len, carry_ref[0],
                            h0_ref, a_scr, b_scr, 0)
    pos, seg_len = _seg_pos(jb, tm, tc, tl)
    carry_ref[1] = _lru_dir(zb_ref[0], next_ref[0], *params, pos, seg_len, carry_ref[1],
                            h1_ref, a_scr, b_scr, 1)


def _lru(zs, conv_w, conv_b, wa_bd, ba, wx_bd, bx, lam, tc, tl):
    b, t, w = zs.shape
    tm = min(ROW_TILE, tc)
    nt = t // tm
    ntc = tc // tm
    nb = tm // SUBLANES
    last = t // SUBLANES - 1
    bwd = lambda j: jnp.where(j < ntc, ntc - 1 - j, nt + ntc - 1 - j)
    full = lambda shape: pl.BlockSpec(shape, lambda bi, j: (0,) * len(shape))
    tile_f = pl.BlockSpec((1, tm, w), lambda bi, j: (bi, j, 0))
    tile_b = pl.BlockSpec((1, tm, w), lambda bi, j: (bi, bwd(j), 0))
    prev = pl.BlockSpec((1, SUBLANES, w), lambda bi, j: (bi, jnp.maximum(j * nb - 1, 0), 0))
    nxt = pl.BlockSpec((1, SUBLANES, w), lambda bi, j: (bi, jnp.minimum((bwd(j) + 1) * nb, last), 0))
    shp = jax.ShapeDtypeStruct((b, t, w), F32)
    return pl.pallas_call(
        functools.partial(_lru_kernel, tc=tc, tl=tl),
        grid=(b, nt),
        in_specs=[tile_f, prev, tile_b, nxt,
                  full(conv_w.shape), full(conv_b.shape), full(wa_bd.shape), full(ba.shape),
                  full(wx_bd.shape), full(bx.shape), full(lam.shape)],
        out_specs=[tile_f, tile_b],
        out_shape=[shp, shp],
        scratch_shapes=[pltpu.VMEM((2, 1, w), F32), pltpu.VMEM((w // LANES, tm, LANES), F32),
                        pltpu.VMEM((w // LANES, tm, LANES), F32)],
        compiler_params=_cparams(("parallel", "arbitrary")),
        name="lru",
    )(zs, zs, zs, zs, conv_w, conv_b, wa_bd, ba, wx_bd, bx, lam)


def _merge_kernel(y0_ref, y1_ref, r_ref, v_ref, ks_ref, g_ref, h_ref, ggb_ref,
                  sga_ref, sgb_ref, x_ref, mod_ref, lng_ref, lnb_ref, rk_ref, e_ref,
                  wr_ref, wl_ref, wo_ref, g2_ref, rt_ref,
                  x1_ref, h2_ref, aff_ref, *, n_experts):
    e = e_ref[...]
    inv = 1.0 / HEAD_DIM
    y = y0_ref[0] + y1_ref[0]
    mean = _mm_split_rhs(y, e) * inv
    yc = y - mean
    var = _mm_split_rhs(yc * yc, e) * inv
    yn = yc * lax.rsqrt(var + LNX_EPS) * lng_ref[...] + lnb_ref[...]
    bonus = _mm_split_rhs(r_ref[0] * ks_ref[0] * rk_ref[...], e) * v_ref[0]
    ya = _mm((yn + bonus) * g_ref[0], wr_ref[...])
    yb = _mm(h_ref[0] * ggb_ref[0], wl_ref[...])
    mix = _mm(sga_ref[0] * ya + sgb_ref[0] * yb, wo_ref[...])
    gt1 = mod_ref[0, 0:1, :]
    sh2 = mod_ref[0, 1:2, :]
    sc2 = mod_ref[0, 2:3, :]
    x1 = x_ref[0] + gt1 * mix
    x1_ref[0] = x1
    xn = x1 * lax.rsqrt(jnp.mean(x1 * x1, axis=-1, keepdims=True) + RMS_EPS) * g2_ref[...]
    h2 = xn * (1.0 + sc2) + sh2
    h2_ref[0] = h2.astype(BF16)
    logits = _mm(h2, rt_ref[...])
    col = lax.broadcasted_iota(jnp.int32, logits.shape, 1)
    logits = jnp.where(col < n_experts, logits, -1e30)
    ex = jnp.exp(logits - jnp.max(logits, axis=-1, keepdims=True))
    aff = ex / jnp.sum(ex, axis=-1, keepdims=True)
    aff_ref[0] = jnp.transpose(aff)[0:n_experts, :]


def _merge(y0, y1, r, v, ks, g, h, ggb, sg, x, mod2, lnx_g, lnx_b, r_k, e_heads,
           wr, wl, wo, g2n, router_pad, tc, n_experts):
    b, tl, d = x.shape
    w = r.shape[-1]
    tm = min(ROW_TILE, tc)
    off = tc // tm
    seq = lambda width, blk=0: pl.BlockSpec((1, tm, width), lambda bi, i: (bi, i + off, blk))
    lat = lambda width: pl.BlockSpec((1, tm, width), lambda bi, i: (bi, i, 0))
    full = lambda shape: pl.BlockSpec(shape, lambda bi, i: (0,) * len(shape))
    return pl.pallas_call(
        functools.partial(_merge_kernel, n_experts=n_experts),
        grid=(b, tl // tm),
        in_specs=[seq(w), seq(w), seq(w), seq(w), seq(w), seq(w), lat(w), seq(w),
                  seq(d, 0), seq(d, 1), lat(d),
                  pl.BlockSpec((1, 3, d), lambda bi, i: (bi, 0, 0)),
                  full((1, w)), full((1, w)), full((1, w)), full(e_heads.shape),
                  full(wr.shape), full(wl.shape), full(wo.shape), full((1, d)),
                  full(router_pad.shape)],
        out_specs=[lat(d), lat(d),
                   pl.BlockSpec((1, n_experts, tm), lambda bi, i: (bi, 0, i))],
        out_shape=[jax.ShapeDtypeStruct((b, tl, d), F32),
                   jax.ShapeDtypeStruct((b, tl, d), BF16),
                   jax.ShapeDtypeStruct((b, n_experts, tl), F32)],
        compiler_params=_cparams(("parallel", "parallel")),
        name="merge",
    )(y0, y1, r, v, ks, g, h, ggb, sg, sg, x, mod2, lnx_g, lnx_b, r_k, e_heads,
      wr, wl, wo, g2n, router_pad)


def _prefix_incl(x, lane):
    n = x.shape[-1]
    s = 1
    while s < n:
        x = x + jnp.where(lane >= s, pltpu.roll(x, s, 1), 0.0)
        s *= 2
    return x


def _route_kernel(aff_ref, slot_ref, *, cap):
    aff = aff_ref[0]
    ne, t = aff.shape
    count_ge = lambda v: jnp.sum(jnp.where(aff >= v, 1.0, 0.0), axis=-1, keepdims=True)

    def bit_step(i, thr):
        cand = thr | jnp.left_shift(jnp.int32(1), 30 - i)
        return jnp.where(count_ge(pltpu.bitcast(cand, F32)) >= cap, cand, thr)

    thr = lax.fori_loop(0, 31, bit_step, jnp.zeros((ne, 1), jnp.int32))
    lo = pltpu.bitcast(thr, F32)
    hi = pltpu.bitcast(jnp.maximum(thr + 1, MIN_NORMAL_BITS), F32)

    def halve(_, lh):
        lo, hi = lh
        mid = lo + 0.5 * (hi - lo)
        ge = count_ge(mid) >= cap
        return jnp.where(ge, mid, lo), jnp.where(ge, hi, mid)

    lo, hi = lax.fori_loop(0, 30, halve, (lo, hi))
    gt = aff >= hi
    eq = jnp.where((aff >= lo) & (aff < hi), 1.0, 0.0)
    need = cap - jnp.sum(jnp.where(gt, 1.0, 0.0), axis=-1, keepdims=True)
    lane = lax.broadcasted_iota(jnp.int32, (ne, t), 1)
    eq_rank = _prefix_incl(eq, lane) - eq
    sel = jnp.where(gt, 1.0, jnp.where(eq_rank < need, eq, 0.0))
    slot = _prefix_incl(sel, lane) - sel
    slot_ref[0] = jnp.where(sel > 0.0, slot, -1.0)


def _route(aff, cap):
    b, ne, t = aff.shape
    spec = pl.BlockSpec((1, ne, t), lambda bi: (bi, 0, 0))
    return pl.pallas_call(
        functools.partial(_route_kernel, cap=cap),
        grid=(b,),
        in_specs=[spec],
        out_specs=spec,
        out_shape=jax.ShapeDtypeStruct((b, ne, t), F32),
        compiler_params=_cparams(("parallel",)),
        name="route",
    )(aff)


def _ffn_kernel(slot_ref, aff_ref, h2_ref, w1_ref, w3_ref, w2_ref, y_ref, *, cap):
    slots = lax.broadcasted_iota(jnp.int32, (cap, 1), 0).astype(F32)
    hit = slot_ref[0, 0] == slots
    gate = jnp.sum(jnp.where(hit, aff_ref[0, 0], 0.0), axis=-1, keepdims=True)
    onehot = jnp.where(hit, 1.0, 0.0).astype(BF16)
    xs = jnp.dot(onehot, h2_ref[0], preferred_element_type=F32).astype(BF16)
    a = jnp.dot(xs, w1_ref[0], preferred_element_type=F32)
    hid = (a * _sigmoid(a)) * jnp.dot(xs, w3_ref[0], preferred_element_type=F32)
    y = jnp.dot(hid.astype(BF16), w2_ref[0], preferred_element_type=F32) * gate
    y_ref[0, 0] = y.astype(BF16)


def _ffn(slot4, aff4, h2, w1, w3, w2, cap):
    b, ne, _, t = slot4.shape
    d = h2.shape[-1]
    f = w1.shape[-1]
    row = pl.BlockSpec((1, 1, 1, t), lambda bi, e: (bi, e, 0, 0))
    return pl.pallas_call(
        functools.partial(_ffn_kernel, cap=cap),
        grid=(b, ne),
        in_specs=[row, row,
                  pl.BlockSpec((1, t, d), lambda bi, e: (bi, 0, 0)),
                  pl.BlockSpec((1, d, f), lambda bi, e: (e, 0, 0)),
                  pl.BlockSpec((1, d, f), lambda bi, e: (e, 0, 0)),
                  pl.BlockSpec((1, f, d), lambda bi, e: (e, 0, 0))],
        out_specs=pl.BlockSpec((1, 1, cap, d), lambda bi, e: (bi, e, 0, 0)),
        out_shape=jax.ShapeDtypeStruct((b, ne, cap, d), BF16),
        compiler_params=_cparams(("parallel", "arbitrary")),
        name="ffn",
    )(slot4, aff4, h2, w1, w3, w2)


def _combine_kernel(slot_ref, y_ref, x1_ref, gt_ref, g_ref, o_ref, acc_ref, *, cap):
    e = pl.program_id(2)

    @pl.when(e == 0)
    def _():
        acc_ref[...] = jnp.zeros_like(acc_ref)

    slots = lax.broadcasted_iota(jnp.int32, (cap, 1), 0).astype(F32)
    onehot = jnp.where(slot_ref[0, 0] == slots, 1.0, 0.0).astype(BF16)
    acc_ref[...] += lax.dot_general(onehot, y_ref[0, 0], (((0,), (0,)), ((), ())),
                                    preferred_element_type=F32)

    @pl.when(e == pl.num_programs(2) - 1)
    def _():
        x2 = x1_ref[0] + gt_ref[0] * acc_ref[...]
        o_ref[0] = x2 * lax.rsqrt(jnp.mean(x2 * x2, axis=-1, keepdims=True) + RMS_EPS) * g_ref[...]


def _combine(slot4, ye, x1, gt2, final_g, cap):
    b, tl, d = x1.shape
    ne = slot4.shape[1]
    tt = min(4 * ROW_TILE, tl)
    row = pl.BlockSpec((1, tt, d), lambda bi, j, e: (bi, j, 0))
    return pl.pallas_call(
        functools.partial(_combine_kernel, cap=cap),
        grid=(b, tl // tt, ne),
        in_specs=[pl.BlockSpec((1, 1, 1, tt), lambda bi, j, e: (bi, e, 0, j)),
                  pl.BlockSpec((1, 1, cap, d), lambda bi, j, e: (bi, e, 0, 0)),
                  row,
                  pl.BlockSpec((1, 1, d), lambda bi, j, e: (bi, 0, 0)),
                  pl.BlockSpec((1, d), lambda bi, j, e: (0, 0))],
        out_specs=row,
        out_shape=jax.ShapeDtypeStruct((b, tl, d), F32),
        scratch_shapes=[pltpu.VMEM((tt, d), F32)],
        compiler_params=_cparams(("parallel", "parallel", "arbitrary")),
        name="combine",
    )(slot4, ye, x1, gt2, final_g)


def _block_diag(w):
    nb, n, _ = w.shape
    eye = jnp.eye(nb, dtype=w.dtype)
    return jnp.einsum("hij,hg->higj", w, eye).reshape(nb * n, nb * n)


def kernel(x, c, ctx, c_ctx, w_mod, b_mod, norm1_g, norm2_g, w_in, mu_rwkv, rwkv_w0, rwkv_w2, rwkv_a0, rwkv_a2, rwkv_g2, rwkv_k_k, rwkv_k_a, rwkv_r_k, lnx_g, lnx_b, w_proj_rwkv, conv_w, conv_b, lru_wa, lru_ba, lru_wx, lru_bx, lru_lam, w_proj_lru, w_out, router, moe_w1, moe_w3, moe_w2, final_g):
    depth = w_mod.shape[0]
    assert depth == 1, "single-layer configuration"
    b, tl, d = x.shape
    tc = ctx.shape[1]
    w = rwkv_k_k.shape[-1]
    lw_ = lru_lam.shape[-1]
    rwkv_cols = mu_rwkv.shape[-1]
    n_experts = router.shape[-1]
    cap = EC_CAPACITY * tl // n_experts
    tm = min(ROW_TILE, tc)
    assert tc % tm == 0 and tl % tm == 0 and tm % CHUNK == 0
    layer = 0

    pad_rows = -(b + 1) % SUBLANES
    c_rows = jnp.concatenate([c, c_ctx[None], jnp.zeros((pad_rows, d), F32)], axis=0)
    m = _modulation(c_rows, w_mod[layer], b_mod[layer])
    m_lat = m[:b].reshape(b, 6, d)
    m_ctx = jnp.broadcast_to(m[b].reshape(1, 6, d), (b, 6, d))
    mod1 = jnp.stack([m_ctx[:, 0:2], m_lat[:, 0:2]], axis=1)
    mod2 = m_lat[:, 2:5]
    gt2 = m_lat[:, 5:6]

    rw, xb, ggb, sg = _inproj(ctx, x, norm1_g[layer][None], mod1, w_in[layer].astype(BF16),
                              rwkv_cols, lw_)

    heads = w // HEAD_DIM
    e_heads = _block_diag(jnp.ones((heads, HEAD_DIM, HEAD_DIM), BF16))
    zpad = jnp.zeros((2, RWKV_W_RANK, w), F32)
    w2p = jnp.concatenate([rwkv_w2[layer], zpad], axis=1)
    a2p = jnp.concatenate([zpad, rwkv_a2[layer]], axis=1)
    r, v, g, ks, ra, yv, ac, sc = _rwkv_front(
        rw, mu_rwkv[layer][None], rwkv_k_k[layer][None], rwkv_k_a[layer][None],
        rwkv_w0[layer], rwkv_a0[layer], w2p, a2p, rwkv_g2[layer], e_heads, tc, tl)
    y0, y1 = _rwkv_carry(ra, yv, ac, sc, tc)

    rows = tl // GRID_W
    xb_lat = xb[:, tc:].reshape(b, rows, GRID_W, lw_).transpose(0, 2, 1, 3).reshape(b, tl, lw_)
    zs = jnp.concatenate([xb[:, :tc], xb_lat], axis=1)
    h0, h1 = _lru(zs, conv_w[layer], conv_b[layer],
                  jax.vmap(_block_diag)(lru_wa[layer]), lru_ba[layer],
                  jax.vmap(_block_diag)(lru_wx[layer]), lru_bx[layer], lru_lam[layer], tc, tl)
    h_lat = (h0[:, tc:] + h1[:, tc:]).reshape(b, GRID_W, rows, lw_).transpose(0, 2, 1, 3).reshape(b, tl, lw_)

    router_pad = jnp.pad(router[layer], ((0, 0), (0, LANES - n_experts)))
    x1, h2, aff = _merge(y0, y1, r, v, ks, g, h_lat, ggb, sg, x, mod2,
                         lnx_g[layer][None], lnx_b[layer][None], rwkv_r_k[layer].reshape(1, w),
                         e_heads, w_proj_rwkv[layer].astype(BF16), w_proj_lru[layer].astype(BF16),
                         w_out[layer].astype(BF16), norm2_g[layer][None], router_pad, tc, n_experts)

    slot4 = _route(aff, cap).reshape(b, n_experts, 1, tl)
    ye = _ffn(slot4, aff.reshape(b, n_experts, 1, tl), h2, moe_w1[layer].astype(BF16),
              moe_w3[layer].astype(BF16), moe_w2[layer].astype(BF16), cap)
    return _combine(slot4, ye, x1, gt2, final_g[None], cap)
```

```python
import functools

import jax
import jax.numpy as jnp
from jax import lax
from jax.experimental import pallas as pl
from jax.experimental.pallas import tpu as pltpu

F32 = jnp.float32
BF16 = jnp.bfloat16

GRID_W = 64
HEAD_DIM = 64
RWKV_W_RANK = 64
RWKV_A_RANK = 64
RWKV_G_RANK = 128
CONV_W = 4
LRU_C = 8.0
EC_CAPACITY = 2
RMS_EPS = 1e-6
LNX_EPS = 64e-5
MIN_NORMAL_BITS = 0x00800000
MIN_NORMAL = 1.1754944e-38

LANES = 128
SUBLANES = 8
CHUNK = 64
ROW_TILE = 256
VMEM_LIMIT = 56 * 1024 * 1024


def _cparams(sem):
    return pltpu.CompilerParams(dimension_semantics=sem, vmem_limit_bytes=VMEM_LIMIT)


def _mm(a, b):
    return jnp.dot(a.astype(BF16), b.astype(BF16), preferred_element_type=F32)


def _split3(a):
    hi = a.astype(BF16)
    r1 = a - hi.astype(F32)
    mid = r1.astype(BF16)
    lo = (r1 - mid.astype(F32)).astype(BF16)
    return hi, mid, lo


def _mm_split_rhs(a, b01):
    hi = a.astype(BF16)
    lo = (a - hi.astype(F32)).astype(BF16)
    d = lambda z: jnp.dot(z, b01, preferred_element_type=F32)
    return d(hi) + d(lo)


def _mm_exact_lhs(a01, b):
    hi, mid, lo = _split3(b)
    d = lambda z: jnp.dot(a01, z, preferred_element_type=F32)
    return d(hi) + d(mid) + d(lo)


def _softplus(z):
    return jnp.maximum(z, 0.0) + jnp.log(1.0 + jnp.exp(-jnp.abs(z)))


def _sigmoid(z):
    return 1.0 / (1.0 + jnp.exp(-z))


def _mod_kernel(c_ref, w_ref, b_ref, o_ref):
    c = c_ref[...]
    s = c * _sigmoid(c)
    o_ref[...] = jnp.dot(s, w_ref[...], precision=lax.Precision.HIGHEST,
                         preferred_element_type=F32) + b_ref[...]


def _modulation(c_rows, w_mod, b_mod):
    rows, d = c_rows.shape
    n = w_mod.shape[1]
    tn = 1536
    return pl.pallas_call(
        _mod_kernel,
        grid=(n // tn,),
        in_specs=[pl.BlockSpec((rows, d), lambda j: (0, 0)),
                  pl.BlockSpec((d, tn), lambda j: (0, j)),
                  pl.BlockSpec((1, tn), lambda j: (0, j))],
        out_specs=pl.BlockSpec((rows, tn), lambda j: (0, j)),
        out_shape=jax.ShapeDtypeStruct((rows, n), F32),
        compiler_params=_cparams(("arbitrary",)),
        name="mod",
    )(c_rows, w_mod, b_mod.reshape(1, n))


def _inproj_kernel(ctx_ref, x_ref, g_ref, mod_ref, w_ref, rw_ref, xb_ref, ggb_ref, sg_ref,
                   *, cols, ntc):
    x = jnp.where(pl.program_id(1) < ntc, ctx_ref[0], x_ref[0])
    xn = x * lax.rsqrt(jnp.mean(x * x, axis=-1, keepdims=True) + RMS_EPS) * g_ref[...]
    sh = mod_ref[0, 0, 0:1, :]
    sc = mod_ref[0, 0, 1:2, :]
    u = (xn * (1.0 + sc) + sh).astype(BF16)
    c_rw, c_xb, c_gb = cols
    d = lambda lo, hi: jnp.dot(u, w_ref[:, lo:hi], preferred_element_type=F32)
    rw_ref[0] = d(0, c_rw)
    xb_ref[0] = d(c_rw, c_xb)
    ggb_ref[0] = jax.nn.gelu(d(c_xb, c_gb))
    sg_ref[0] = _sigmoid(d(c_gb, w_ref.shape[1]))


def _inproj(ctx, x, g1, mod1, w_in_bf, rwkv_cols, lru_w):
    b, tc, d = ctx.shape
    t = tc + x.shape[1]
    n = w_in_bf.shape[1]
    tm = min(ROW_TILE, tc)
    ntc = tc // tm
    cols = (rwkv_cols, rwkv_cols + lru_w, rwkv_cols + 2 * lru_w)
    gate_cols = n - cols[2]
    row_spec = lambda w: pl.BlockSpec((1, tm, w), lambda bi, i: (bi, i, 0))
    return pl.pallas_call(
        functools.partial(_inproj_kernel, cols=cols, ntc=ntc),
        grid=(b, t // tm),
        in_specs=[pl.BlockSpec((1, tm, d), lambda bi, i: (bi, jnp.minimum(i, ntc - 1), 0)),
                  pl.BlockSpec((1, tm, d), lambda bi, i: (bi, jnp.maximum(i - ntc, 0), 0)),
                  pl.BlockSpec((1, d), lambda bi, i: (0, 0)),
                  pl.BlockSpec((1, 1, 2, d), lambda bi, i: (bi, jnp.where(i < ntc, 0, 1), 0, 0)),
                  pl.BlockSpec((d, n), lambda bi, i: (0, 0))],
        out_specs=[row_spec(rwkv_cols), row_spec(lru_w), row_spec(lru_w), row_spec(gate_cols)],
        out_shape=[jax.ShapeDtypeStruct((b, t, rwkv_cols), F32),
                   jax.ShapeDtypeStruct((b, t, lru_w), F32),
                   jax.ShapeDtypeStruct((b, t, lru_w), F32),
                   jax.ShapeDtypeStruct((b, t, gate_cols), F32)],
        compiler_params=_cparams(("parallel", "parallel")),
        name="inproj",
    )(ctx, x, g1, mod1, w_in_bf)


def _shift_down(z, halo_prev, s):
    tm = z.shape[0]
    rolled = pltpu.roll(z, s, 0)
    hp = pltpu.roll(halo_prev, s, 0)
    r8 = lax.broadcasted_iota(jnp.int32, (SUBLANES, 1), 0)
    head = jnp.where(r8 < s, hp, rolled[0:SUBLANES])
    return jnp.concatenate([head, rolled[SUBLANES:tm]], axis=0)


def _shift_up(z, halo_next, s):
    tm = z.shape[0]
    rolled = pltpu.roll(z, tm - s, 0)
    hn = pltpu.roll(halo_next, SUBLANES - s, 0)
    r8 = lax.broadcasted_iota(jnp.int32, (SUBLANES, 1), 0)
    tail = jnp.where(r8 >= SUBLANES - s, hn, rolled[tm - SUBLANES:tm])
    return jnp.concatenate([rolled[0:tm - SUBLANES], tail], axis=0)


def _seg_pos(i, tm, tc, tl):
    grow = lax.broadcasted_iota(jnp.int32, (tm, 1), 0) + i * tm
    in_ctx = grow < tc
    pos = jnp.where(in_ctx, grow, grow - tc)
    seg_len = jnp.where(in_ctx, tc, tl)
    return pos, seg_len


def _halo_specs(tm, t, width):
    nb = tm // SUBLANES
    last = t // SUBLANES - 1
    prev = pl.BlockSpec((1, SUBLANES, width), lambda bi, i: (bi, jnp.maximum(i * nb - 1, 0), 0))
    nxt = pl.BlockSpec((1, SUBLANES, width), lambda bi, i: (bi, jnp.minimum((i + 1) * nb, last), 0))
    return prev, nxt


def _chunk_masks(rev):
    c = CHUNK
    n = 2 * c
    sign = -1 if rev else 1
    ti = lax.broadcasted_iota(jnp.int32, (c, c), 0)
    ii = lax.broadcasted_iota(jnp.int32, (c, c), 1)
    row = lax.broadcasted_iota(jnp.int32, (c, n), 0)
    idx = lax.broadcasted_iota(jnp.int32, (c, n), 1) % c
    delta = (idx - row) * sign
    lane = lax.broadcasted_iota(jnp.int32, (1, n), 1)
    return {
        "cum": jnp.where((ii - ti) * sign <= 0, 1.0, 0.0).astype(BF16),
        "strict": delta < 0,
        "incl": delta <= 0,
        "eye": jnp.where(idx == row, 1.0, 0.0).astype(F32),
        "m0": jnp.where(lane < c, 1.0, 0.0).astype(BF16),
        "m1": jnp.where(lane >= c, 1.0, 0.0).astype(BF16),
    }


def _chunk_group(units, m, nsq):
    c = CHUNK
    n = 2 * c
    bf = lambda z: z.astype(BF16)
    dot = lambda x, y: jnp.dot(x, y, preferred_element_type=F32)
    m0, m1 = m["m0"], m["m1"]
    blockdiag = lambda zb: jnp.concatenate([zb * m0, zb * m1], axis=0)

    def head_t(z):
        zt = jnp.transpose(z)
        return jnp.concatenate([zt[0:c], zt[c:n]], axis=1)

    lcs = [_mm_exact_lhs(m["cum"], u[3]) for u in units]
    pre = []
    for (r, v, kk, lw, kd, bd), lc in zip(units, lcs):
        ltot = jnp.sum(lw, axis=0, keepdims=True)
        en = jnp.exp(-lc)
        eh = jnp.exp(ltot - lc)
        qr = r * jnp.exp(lc)
        qa = bf(-kk * jnp.exp(lc - lw))
        v_bd = blockdiag(bf(v))
        pre.append(dict(q=jnp.concatenate([qa, bf(qr)], axis=0), qa_bd=blockdiag(qa), qr=qr,
                        kbx=jnp.concatenate([blockdiag(bf(bd * en)), blockdiag(bf(kd * en))], axis=0),
                        v_bd=v_bd, v0=jnp.concatenate([jnp.zeros_like(v_bd), v_bd], axis=1),
                        kt=bf(jnp.concatenate([head_t(bd * eh), head_t(kd * eh)], axis=1)),
                        dg=m["eye"] * jnp.exp(ltot)))

    gram = [lax.dot_general(p["q"], p["kbx"], (((1,), (1,)), ((), ())), preferred_element_type=F32)
            for p in pre]
    gab_f = [jnp.where(m["strict"], g[:c, :n], 0.0) for g in gram]
    gak = [bf(jnp.where(m["strict"], g[:c, n:], 0.0)) for g in gram]
    grbk = [bf(jnp.concatenate([jnp.where(m["incl"], g[c:, :n], 0.0),
                                jnp.where(m["incl"], g[c:, n:], 0.0)], axis=1)) for g in gram]

    t_inv = [m["eye"] + g for g in gab_f]
    gp = [bf(g) for g in gab_f]
    gp = [bf(dot(g, blockdiag(g))) for g in gp]
    for _ in range(nsq - 1):
        prod = [dot(jnp.concatenate([bf(t), g], axis=0), blockdiag(g)) for t, g in zip(t_inv, gp)]
        t_inv = [t + p[:c] for t, p in zip(t_inv, prod)]
        gp = [bf(p[c:]) for p in prod]
    t_inv = [bf(t + dot(bf(t), blockdiag(g))) for t, g in zip(t_inv, gp)]

    gv = [bf(dot(g, p["v_bd"])) for g, p in zip(gak, pre)]
    tw = [bf(dot(t, jnp.concatenate([p["qa_bd"], blockdiag(x)], axis=1)))
          for t, p, x in zip(t_inv, pre, gv)]
    big = [dot(jnp.concatenate([g, p["kt"]], axis=0),
               jnp.concatenate([jnp.concatenate([blockdiag(x[:, :n]), blockdiag(x[:, n:])], axis=1),
                                p["v0"]], axis=0))
           for g, p, x in zip(grbk, pre, tw)]
    return [(p["qr"] + o[:c, :n], o[:c, n:], p["dg"] + o[c:, :n], o[c:, n:])
            for p, o in zip(pre, big)]


def _front_kernel(rw_ref, prev_ref, next_ref, mu_ref, kkw_ref, ka_ref, w0_ref, a0_ref,
                  w2_ref, a2_ref, g2_ref, e_ref,
                  r_ref, v_ref, g_ref, ks_ref, ra_ref, yv_ref, ac_ref, sc_ref,
                  *, tc, tl, width, nsq):
    i = pl.program_id(1)
    p = rw_ref[0]
    tm = p.shape[0]
    pos, seg_len = _seg_pos(i, tm, tc, tl)
    prev = jnp.where(pos >= 1, _shift_down(p, prev_ref[0], 1), 0.0)
    nxt = jnp.where(pos + 1 < seg_len, _shift_up(p, next_ref[0], 1), 0.0)
    pm = p + mu_ref[...] * (0.5 * (prev + nxt) - p)

    w = width
    r = pm[:, 0:w]
    k = pm[:, w:2 * w]
    v = pm[:, 2 * w:3 * w]
    wa = pm[:, 3 * w:3 * w + RWKV_W_RANK + RWKV_A_RANK]
    gd = pm[:, 3 * w + RWKV_W_RANK + RWKV_A_RANK:]

    kkr = k * kkw_ref[...]
    ss = _mm_split_rhs(kkr * kkr, e_ref[...])
    kk = kkr / jnp.maximum(jnp.sqrt(ss), 1e-12)
    r_ref[0] = r
    v_ref[0] = v
    g_ref[0] = _mm(_sigmoid(gd), g2_ref[...])

    twa = jnp.tanh(wa)
    ka = ka_ref[...]
    per_dir = []
    ksum = jnp.zeros_like(k)
    for d in range(2):
        wpre = w0_ref[d:d + 1, :] + _mm(twa, w2_ref[d])
        lw = -jnp.exp(-_softplus(-wpre) - 0.5)
        eta = _sigmoid(a0_ref[d:d + 1, :] + _mm(wa, a2_ref[d]))
        kd = k * (1.0 + (eta - 1.0) * ka)
        per_dir.append((lw, kd, kk * eta))
        ksum = ksum + kd
    ks_ref[0] = ksum

    n = 2 * CHUNK
    for d, (lw, kd, bd) in enumerate(per_dir):
        m = _chunk_masks(d == 1)
        keys, units = [], []
        for ci in range(tm // CHUNK):
            rows = slice(ci * CHUNK, (ci + 1) * CHUNK)
            for hp in range(w // n):
                lanes = slice(hp * n, (hp + 1) * n)
                keys.append((ci, hp, rows, lanes))
                units.append((r[rows, lanes], v[rows, lanes], kk[rows, lanes],
                              lw[rows, lanes], kd[rows, lanes], bd[rows, lanes]))
        for (ci, hp, rows, lanes), (ra, yv, ac, sc) in zip(keys, _chunk_group(units, m, nsq)):
            ra_ref[d, 0, rows, lanes] = ra
            yv_ref[d, 0, rows, lanes] = yv
            ac_ref[d, 0, ci, hp] = ac
            sc_ref[d, 0, ci, hp] = sc


def _rwkv_front(rw, mu, k_k, k_a, w0, a0, w2p, a2p, g2, e_heads, tc, tl):
    b, t, cols = rw.shape
    w = k_k.shape[-1]
    tm = min(ROW_TILE, tc)
    n = 2 * CHUNK
    nc = t // CHUNK
    cpt = tm // CHUNK
    nsq = (CHUNK - 1).bit_length() - 1
    prev_spec, next_spec = _halo_specs(tm, t, cols)
    full = lambda shape: pl.BlockSpec(shape, lambda bi, i: (0,) * len(shape))
    row = pl.BlockSpec((1, tm, w), lambda bi, i: (bi, i, 0))
    drow = pl.BlockSpec((2, 1, tm, w), lambda bi, i: (0, bi, i, 0))
    mat = pl.BlockSpec((2, 1, cpt, w // n, CHUNK, n), lambda bi, i: (0, bi, i, 0, 0, 0))
    shp = jax.ShapeDtypeStruct((b, t, w), F32)
    dshp = jax.ShapeDtypeStruct((2, b, t, w), F32)
    mshp = jax.ShapeDtypeStruct((2, b, nc, w // n, CHUNK, n), F32)
    return pl.pallas_call(
        functools.partial(_front_kernel, tc=tc, tl=tl, width=w, nsq=nsq),
        grid=(b, t // tm),
        in_specs=[pl.BlockSpec((1, tm, cols), lambda bi, i: (bi, i, 0)), prev_spec, next_spec,
                  full((1, cols)), full((1, w)), full((1, w)), full((2, w)), full((2, w)),
                  full(w2p.shape), full(a2p.shape), full(g2.shape), full(e_heads.shape)],
        out_specs=[row, row, row, row, drow, drow, mat, mat],
        out_shape=[shp, shp, shp, shp, dshp, dshp, mshp, mshp],
        compiler_params=_cparams(("parallel", "parallel")),
        name="front",
    )(rw, rw, rw, mu, k_k, k_a, w0, a0, w2p, a2p, g2, e_heads)


def _carry_kernel(ra0, yv0, ac0, sc0, ra1, yv1, ac1, sc1, y0_ref, y1_ref, st_ref):
    j = pl.program_id(0)

    @pl.when(j == 0)
    def _():
        st_ref[...] = jnp.zeros_like(st_ref)

    nb, c, w = y0_ref.shape
    n = 2 * c
    lane = lax.broadcasted_iota(jnp.int32, (1, n), 1)
    m0 = jnp.where(lane < c, 1.0, 0.0).astype(BF16)
    m1 = jnp.where(lane >= c, 1.0, 0.0).astype(BF16)
    blockdiag = lambda zb: jnp.concatenate([zb * m0, zb * m1], axis=0)
    dot = lambda x, y: jnp.dot(x, y, preferred_element_type=F32)

    srcs = ((ra0, yv0, ac0, sc0, y0_ref), (ra1, yv1, ac1, sc1, y1_ref))
    units = [(d, bi, hp) for d in range(2) for bi in range(nb) for hp in range(w // n)]
    lanes = lambda hp: slice(hp * n, (hp + 1) * n)
    sts = [st_ref[d, bi, hp] for d, bi, hp in units]
    st_hi = [s.astype(BF16) for s in sts]
    st_lo = [(s - h.astype(F32)).astype(BF16) for s, h in zip(sts, st_hi)]
    bd_hi = [blockdiag(h) for h in st_hi]
    bd_lo = [blockdiag(l) for l in st_lo]
    ys = [dot(srcs[d][0][0, bi, :, lanes(hp)].astype(BF16), h) for (d, bi, hp), h in zip(units, bd_hi)]
    for (d, bi, hp), y in zip(units, ys):
        srcs[d][4][bi, :, lanes(hp)] = y + srcs[d][1][0, bi, :, lanes(hp)]
    acs = [srcs[d][2][0, bi, 0, hp] for d, bi, hp in units]
    ac_hi = [a.astype(BF16) for a in acs]
    ac_lo = [(a - h.astype(F32)).astype(BF16) for a, h in zip(acs, ac_hi)]
    news = [dot(ah, bh) + dot(al, bh) + dot(ah, bl)
            for ah, al, bh, bl in zip(ac_hi, ac_lo, bd_hi, bd_lo)]
    for (d, bi, hp), new in zip(units, news):
        st_ref[d, bi, hp] = new + srcs[d][3][0, bi, 0, hp]


def _rwkv_carry(ra, yv, ac, sc, tc):
    _, b, t, w = ra.shape
    n = 2 * CHUNK
    nc = t // CHUNK
    ncc = tc // CHUNK
    fwd = lambda j: j
    bwd = lambda j: jnp.where(j < ncc, ncc - 1 - j, nc + ncc - 1 - j)
    row = lambda d, f: pl.BlockSpec((1, b, CHUNK, w), lambda j: (d, 0, f(j), 0))
    mat = lambda d, f: pl.BlockSpec((1, b, 1, w // n, CHUNK, n), lambda j: (d, 0, f(j), 0, 0, 0))
    out = lambda f: pl.BlockSpec((b, CHUNK, w), lambda j: (0, f(j), 0))
    shp = jax.ShapeDtypeStruct((b, t, w), F32)
    return pl.pallas_call(
        _carry_kernel,
        grid=(nc,),
        in_specs=[row(0, fwd), row(0, fwd), mat(0, fwd), mat(0, fwd),
                  row(1, bwd), row(1, bwd), mat(1, bwd), mat(1, bwd)],
        out_specs=[out(fwd), out(bwd)],
        out_shape=[shp, shp],
        scratch_shapes=[pltpu.VMEM((2, b, w // n, CHUNK, n), F32)],
        compiler_params=_cparams(("arbitrary",)),
        name="carry",
    )(ra, yv, ac, sc, ra, yv, ac, sc)


def _lru_dir(z, halo, cw_ref, cb_ref, wa_ref, ba_ref, wx_ref, bx_ref, lam_ref, pos, seg_len,
             carry, h_ref, a_scr, b_scr, d):
    rev = d == 1
    tm, w = z.shape
    xc = cb_ref[d:d + 1, :] + cw_ref[d, CONV_W - 1:CONV_W, :] * z
    for s in range(1, CONV_W):
        if rev:
            zs = jnp.where(pos + s < seg_len, _shift_up(z, halo, s), 0.0)
        else:
            zs = jnp.where(pos >= s, _shift_down(z, halo, s), 0.0)
        xc = xc + cw_ref[d, CONV_W - 1 - s:CONV_W - s, :] * zs
    rg = _sigmoid(_mm(xc, wa_ref[d]) + ba_ref[d:d + 1, :])
    ig = _sigmoid(_mm(xc, wx_ref[d]) + bx_ref[d:d + 1, :])
    log_a = -LRU_C * rg * _softplus(-lam_ref[d:d + 1, :])
    a = jnp.exp(log_a)
    q = -jnp.tanh(log_a) * (a * a + 1.0)
    bv = q * lax.rsqrt(jnp.maximum(q, MIN_NORMAL)) * (ig * xc)

    groups = tm // SUBLANES
    nl = w // LANES
    for kl in range(nl):
        a_scr[kl] = a[:, kl * LANES:(kl + 1) * LANES]
        b_scr[kl] = bv[:, kl * LANES:(kl + 1) * LANES]
    rows_of = lambda ref, r: jnp.concatenate(
        [ref[kl, pl.ds(r, groups, stride=SUBLANES), :] for kl in range(nl)], axis=1)
    order = list(reversed(range(SUBLANES))) if rev else list(range(SUBLANES))
    acum, bcum = {}, {}
    pa = pb = None
    for r in order:
        ar, br = rows_of(a_scr, r), rows_of(b_scr, r)
        pa, pb = (ar, br) if pa is None else (ar * pa, br + ar * pb)
        acum[r], bcum[r] = pa, pb
    cin = [None] * groups
    for gi in (reversed(range(groups)) if rev else range(groups)):
        cin[gi] = carry
        carry = pa[gi:gi + 1] * carry + pb[gi:gi + 1]
    cin = jnp.concatenate(cin, axis=0)
    for r in order:
        hr = acum[r] * cin + bcum[r]
        for kl in range(nl):
            a_scr[kl, pl.ds(r, groups, stride=SUBLANES), :] = hr[:, kl * LANES:(kl + 1) * LANES]
    for kl in range(nl):
        h_ref[0, :, kl * LANES:(kl + 1) * LANES] = a_scr[kl]
    return carry


def _lru_kernel(zf_ref, prev_ref, zb_ref, next_ref, cw_ref, cb_ref, wa_ref, ba_ref, wx_ref, bx_ref,
                lam_ref, h0_ref, h1_ref, carry_ref, a_scr, b_scr, *, tc, tl):
    j = pl.program_id(1)

    @pl.when(j == 0)
    def _():
        carry_ref[...] = jnp.zeros_like(carry_ref)

    tm = zf_ref.shape[1]
    nt = (tc + tl) // tm
    ntc = tc // tm
    jb = jnp.where(j < ntc, ntc - 1 - j, nt + ntc - 1 - j)
    params = (cw_ref, cb_ref, wa_ref, ba_ref, wx_ref, bx_ref, lam_ref)
    pos, seg_len = _seg_pos(j, tm, tc, tl)
    carry_ref[0] = _lru_dir(zf_ref[0], prev_ref[0], *params, pos, seg_len, carry_ref[0],
                            h0_ref, a_scr, b_scr, 0)
    pos, seg_len = _seg_pos(jb, tm, tc, tl)
    carry_ref[1] = _lru_dir(zb_ref[0], next_ref[0], *params, pos, seg_len, carry_ref[1],
                            h1_ref, a_scr, b_scr, 1)


def _lru(zs, conv_w, conv_b, wa_bd, ba, wx_bd, bx, lam, tc, tl):
    b, t, w = zs.shape
    tm = min(ROW_TILE, tc)
    nt = t // tm
    ntc = tc // tm
    nb = tm // SUBLANES
    last = t // SUBLANES - 1
    bwd = lambda j: jnp.where(j < ntc, ntc - 1 - j, nt + ntc - 1 - j)
    full = lambda shape: pl.BlockSpec(shape, lambda bi, j: (0,) * len(shape))
    tile_f = pl.BlockSpec((1, tm, w), lambda bi, j: (bi, j, 0))
    tile_b = pl.BlockSpec((1, tm, w), lambda bi, j: (bi, bwd(j), 0))
    prev = pl.BlockSpec((1, SUBLANES, w), lambda bi, j: (bi, jnp.maximum(j * nb - 1, 0), 0))
    nxt = pl.BlockSpec((1, SUBLANES, w), lambda bi, j: (bi, jnp.minimum((bwd(j) + 1) * nb, last), 0))
    shp = jax.ShapeDtypeStruct((b, t, w), F32)
    return pl.pallas_call(
        functools.partial(_lru_kernel, tc=tc, tl=tl),
        grid=(b, nt),
        in_specs=[tile_f, prev, tile_b, nxt,
                  full(conv_w.shape), full(conv_b.shape), full(wa_bd.shape), full(ba.shape),
                  full(wx_bd.shape), full(bx.shape), full(lam.shape)],
        out_specs=[tile_f, tile_b],
        out_shape=[shp, shp],
        scratch_shapes=[pltpu.VMEM((2, 1, w), F32), pltpu.VMEM((w // LANES, tm, LANES), F32),
                        pltpu.VMEM((w // LANES, tm, LANES), F32)],
        compiler_params=_cparams(("parallel", "arbitrary")),
        name="lru",
    )(zs, zs, zs, zs, conv_w, conv_b, wa_bd, ba, wx_bd, bx, lam)


def _merge_kernel(y0_ref, y1_ref, r_ref, v_ref, ks_ref, g_ref, h_ref, ggb_ref,
                  sga_ref, sgb_ref, x_ref, mod_ref, lng_ref, lnb_ref, rk_ref, e_ref,
                  wr_ref, wl_ref, wo_ref, g2_ref, rt_ref,
                  x1_ref, h2_ref, aff_ref, *, n_experts):
    e = e_ref[...]
    inv = 1.0 / HEAD_DIM
    y = y0_ref[0] + y1_ref[0]
    mean = _mm_split_rhs(y, e) * inv
    yc = y - mean
    var = _mm_split_rhs(yc * yc, e) * inv
    yn = yc * lax.rsqrt(var + LNX_EPS) * lng_ref[...] + lnb_ref[...]
    bonus = _mm_split_rhs(r_ref[0] * ks_ref[0] * rk_ref[...], e) * v_ref[0]
    ya = _mm((yn + bonus) * g_ref[0], wr_ref[...])
    yb = _mm(h_ref[0] * ggb_ref[0], wl_ref[...])
    mix = _mm(sga_ref[0] * ya + sgb_ref[0] * yb, wo_ref[...])
    gt1 = mod_ref[0, 0:1, :]
    sh2 = mod_ref[0, 1:2, :]
    sc2 = mod_ref[0, 2:3, :]
    x1 = x_ref[0] + gt1 * mix
    x1_ref[0] = x1
    xn = x1 * lax.rsqrt(jnp.mean(x1 * x1, axis=-1, keepdims=True) + RMS_EPS) * g2_ref[...]
    h2 = xn * (1.0 + sc2) + sh2
    h2_ref[0] = h2.astype(BF16)
    logits = _mm(h2, rt_ref[...])
    col = lax.broadcasted_iota(jnp.int32, logits.shape, 1)
    logits = jnp.where(col < n_experts, logits, -1e30)
    ex = jnp.exp(logits - jnp.max(logits, axis=-1, keepdims=True))
    aff = ex / jnp.sum(ex, axis=-1, keepdims=True)
    aff_ref[0] = jnp.transpose(aff)[0:n_experts, :]


def _merge(y0, y1, r, v, ks, g, h, ggb, sg, x, mod2, lnx_g, lnx_b, r_k, e_heads,
           wr, wl, wo, g2n, router_pad, tc, n_experts):
    b, tl, d = x.shape
    w = r.shape[-1]
    tm = min(ROW_TILE, tc)
    off = tc // tm
    seq = lambda width, blk=0: pl.BlockSpec((1, tm, width), lambda bi, i: (bi, i + off, blk))
    lat = lambda width: pl.BlockSpec((1, tm, width), lambda bi, i: (bi, i, 0))
    full = lambda shape: pl.BlockSpec(shape, lambda bi, i: (0,) * len(shape))
    return pl.pallas_call(
        functools.partial(_merge_kernel, n_experts=n_experts),
        grid=(b, tl // tm),
        in_specs=[seq(w), seq(w), seq(w), seq(w), seq(w), seq(w), lat(w), seq(w),
                  seq(d, 0), seq(d, 1), lat(d),
                  pl.BlockSpec((1, 3, d), lambda bi, i: (bi, 0, 0)),
                  full((1, w)), full((1, w)), full((1, w)), full(e_heads.shape),
                  full(wr.shape), full(wl.shape), full(wo.shape), full((1, d)),
                  full(router_pad.shape)],
        out_specs=[lat(d), lat(d),
                   pl.BlockSpec((1, n_experts, tm), lambda bi, i: (bi, 0, i))],
        out_shape=[jax.ShapeDtypeStruct((b, tl, d), F32),
                   jax.ShapeDtypeStruct((b, tl, d), BF16),
                   jax.ShapeDtypeStruct((b, n_experts, tl), F32)],
        compiler_params=_cparams(("parallel", "parallel")),
        name="merge",
    )(y0, y1, r, v, ks, g, h, ggb, sg, sg, x, mod2, lnx_g, lnx_b, r_k, e_heads,
      wr, wl, wo, g2n, router_pad)


def _prefix_incl(x, lane):
    n = x.shape[-1]
    s = 1
    while s < n:
        x = x + jnp.where(lane >= s, pltpu.roll(x, s, 1), 0.0)
        s *= 2
    return x


def _route_kernel(aff_ref, slot_ref, *, cap):
    aff = aff_ref[0]
    ne, t = aff.shape
    count_ge = lambda v: jnp.sum(jnp.where(aff >= v, 1.0, 0.0), axis=-1, keepdims=True)

    def bit_step(i, thr):
        cand = thr | jnp.left_shift(jnp.int32(1), 30 - i)
        return jnp.where(count_ge(pltpu.bitcast(cand, F32)) >= cap, cand, thr)

    thr = lax.fori_loop(0, 31, bit_step, jnp.zeros((ne, 1), jnp.int32))
    lo = pltpu.bitcast(thr, F32)
    hi = pltpu.bitcast(jnp.maximum(thr + 1, MIN_NORMAL_BITS), F32)

    def halve(_, lh):
        lo, hi = lh
        mid = lo + 0.5 * (hi - lo)
        ge = count_ge(mid) >= cap
        return jnp.where(ge, mid, lo), jnp.where(ge, hi, mid)

    lo, hi = lax.fori_loop(0, 30, halve, (lo, hi))
    gt = aff >= hi
    eq = jnp.where((aff >= lo) & (aff < hi), 1.0, 0.0)
    need = cap - jnp.sum(jnp.where(gt, 1.0, 0.0), axis=-1, keepdims=True)
    lane = lax.broadcasted_iota(jnp.int32, (ne, t), 1)
    eq_rank = _prefix_incl(eq, lane) - eq
    sel = jnp.where(gt, 1.0, jnp.where(eq_rank < need, eq, 0.0))
    slot = _prefix_incl(sel, lane) - sel
    slot_ref[0] = jnp.where(sel > 0.0, slot, -1.0)


def _route(aff, cap):
    b, ne, t = aff.shape
    spec = pl.BlockSpec((1, ne, t), lambda bi: (bi, 0, 0))
    return pl.pallas_call(
        functools.partial(_route_kernel, cap=cap),
        grid=(b,),
        in_specs=[spec],
        out_specs=spec,
        out_shape=jax.ShapeDtypeStruct((b, ne, t), F32),
        compiler_params=_cparams(("parallel",)),
        name="route",
    )(aff)


def _ffn_kernel(slot_ref, aff_ref, h2_ref, w1_ref, w3_ref, w2_ref, y_ref, *, cap):
    slots = lax.broadcasted_iota(jnp.int32, (cap, 1), 0).astype(F32)
    hit = slot_ref[0, 0] == slots
    gate = jnp.sum(jnp.where(hit, aff_ref[0, 0], 0.0), axis=-1, keepdims=True)
    onehot = jnp.where(hit, 1.0, 0.0).astype(BF16)
    xs = jnp.dot(onehot, h2_ref[0], preferred_element_type=F32).astype(BF16)
    a = jnp.dot(xs, w1_ref[0], preferred_element_type=F32)
    hid = (a * _sigmoid(a)) * jnp.dot(xs, w3_ref[0], preferred_element_type=F32)
    y = jnp.dot(hid.astype(BF16), w2_ref[0], preferred_element_type=F32) * gate
    y_ref[0, 0] = y.astype(BF16)


def _ffn(slot4, aff4, h2, w1, w3, w2, cap):
    b, ne, _, t = slot4.shape
    d = h2.shape[-1]
    f = w1.shape[-1]
    row = pl.BlockSpec((1, 1, 1, t), lambda bi, e: (bi, e, 0, 0))
    return pl.pallas_call(
        functools.partial(_ffn_kernel, cap=cap),
        grid=(b, ne),
        in_specs=[row, row,
                  pl.BlockSpec((1, t, d), lambda bi, e: (bi, 0, 0)),
                  pl.BlockSpec((1, d, f), lambda bi, e: (e, 0, 0)),
                  pl.BlockSpec((1, d, f), lambda bi, e: (e, 0, 0)),
                  pl.BlockSpec((1, f, d), lambda bi, e: (e, 0, 0))],
        out_specs=pl.BlockSpec((1, 1, cap, d), lambda bi, e: (bi, e, 0, 0)),
        out_shape=jax.ShapeDtypeStruct((b, ne, cap, d), BF16),
        compiler_params=_cparams(("parallel", "arbitrary")),
        name="ffn",
    )(slot4, aff4, h2, w1, w3, w2)


def _combine_kernel(slot_ref, y_ref, x1_ref, gt_ref, g_ref, o_ref, acc_ref, *, cap):
    e = pl.program_id(2)

    @pl.when(e == 0)
    def _():
        acc_ref[...] = jnp.zeros_like(acc_ref)

    slots = lax.broadcasted_iota(jnp.int32, (cap, 1), 0).astype(F32)
    onehot = jnp.where(slot_ref[0, 0] == slots, 1.0, 0.0).astype(BF16)
    acc_ref[...] += lax.dot_general(onehot, y_ref[0, 0], (((0,), (0,)), ((), ())),
                                    preferred_element_type=F32)

    @pl.when(e == pl.num_programs(2) - 1)
    def _():
        x2 = x1_ref[0] + gt_ref[0] * acc_ref[...]
        o_ref[0] = x2 * lax.rsqrt(jnp.mean(x2 * x2, axis=-1, keepdims=True) + RMS_EPS) * g_ref[...]


def _combine(slot4, ye, x1, gt2, final_g, cap):
    b, tl, d = x1.shape
    ne = slot4.shape[1]
    tt = min(8 * ROW_TILE, tl)
    row = pl.BlockSpec((1, tt, d), lambda bi, j, e: (bi, j, 0))
    return pl.pallas_call(
        functools.partial(_combine_kernel, cap=cap),
        grid=(b, tl // tt, ne),
        in_specs=[pl.BlockSpec((1, 1, 1, tt), lambda bi, j, e: (bi, e, 0, j)),
                  pl.BlockSpec((1, 1, cap, d), lambda bi, j, e: (bi, e, 0, 0)),
                  row,
                  pl.BlockSpec((1, 1, d), lambda bi, j, e: (bi, 0, 0)),
                  pl.BlockSpec((1, d), lambda bi, j, e: (0, 0))],
        out_specs=row,
        out_shape=jax.ShapeDtypeStruct((b, tl, d), F32),
        scratch_shapes=[pltpu.VMEM((tt, d), F32)],
        compiler_params=_cparams(("parallel", "parallel", "arbitrary")),
        name="combine",
    )(slot4, ye, x1, gt2, final_g)


def _block_diag(w):
    nb, n, _ = w.shape
    eye = jnp.eye(nb, dtype=w.dtype)
    return jnp.einsum("hij,hg->higj", w, eye).reshape(nb * n, nb * n)


def kernel(x, c, ctx, c_ctx, w_mod, b_mod, norm1_g, norm2_g, w_in, mu_rwkv, rwkv_w0, rwkv_w2, rwkv_a0, rwkv_a2, rwkv_g2, rwkv_k_k, rwkv_k_a, rwkv_r_k, lnx_g, lnx_b, w_proj_rwkv, conv_w, conv_b, lru_wa, lru_ba, lru_wx, lru_bx, lru_lam, w_proj_lru, w_out, router, moe_w1, moe_w3, moe_w2, final_g):
    depth = w_mod.shape[0]
    assert depth == 1, "single-layer configuration"
    b, tl, d = x.shape
    tc = ctx.shape[1]
    w = rwkv_k_k.shape[-1]
    lw_ = lru_lam.shape[-1]
    rwkv_cols = mu_rwkv.shape[-1]
    n_experts = router.shape[-1]
    cap = EC_CAPACITY * tl // n_experts
    tm = min(ROW_TILE, tc)
    assert tc % tm == 0 and tl % tm == 0 and tm % CHUNK == 0
    layer = 0

    pad_rows = -(b + 1) % SUBLANES
    c_rows = jnp.concatenate([c, c_ctx[None], jnp.zeros((pad_rows, d), F32)], axis=0)
    m = _modulation(c_rows, w_mod[layer], b_mod[layer])
    m_lat = m[:b].reshape(b, 6, d)
    m_ctx = jnp.broadcast_to(m[b].reshape(1, 6, d), (b, 6, d))
    mod1 = jnp.stack([m_ctx[:, 0:2], m_lat[:, 0:2]], axis=1)
    mod2 = m_lat[:, 2:5]
    gt2 = m_lat[:, 5:6]

    rw, xb, ggb, sg = _inproj(ctx, x, norm1_g[layer][None], mod1, w_in[layer].astype(BF16),
                              rwkv_cols, lw_)

    heads = w // HEAD_DIM
    e_heads = _block_diag(jnp.ones((heads, HEAD_DIM, HEAD_DIM), BF16))
    zpad = jnp.zeros((2, RWKV_W_RANK, w), F32)
    w2p = jnp.concatenate([rwkv_w2[layer], zpad], axis=1)
    a2p = jnp.concatenate([zpad, rwkv_a2[layer]], axis=1)
    r, v, g, ks, ra, yv, ac, sc = _rwkv_front(
        rw, mu_rwkv[layer][None], rwkv_k_k[layer][None], rwkv_k_a[layer][None],
        rwkv_w0[layer], rwkv_a0[layer], w2p, a2p, rwkv_g2[layer], e_heads, tc, tl)
    y0, y1 = _rwkv_carry(ra, yv, ac, sc, tc)

    rows = tl // GRID_W
    xb_lat = xb[:, tc:].reshape(b, rows, GRID_W, lw_).transpose(0, 2, 1, 3).reshape(b, tl, lw_)
    zs = jnp.concatenate([xb[:, :tc], xb_lat], axis=1)
    h0, h1 = _lru(zs, conv_w[layer], conv_b[layer],
                  jax.vmap(_block_diag)(lru_wa[layer]), lru_ba[layer],
                  jax.vmap(_block_diag)(lru_wx[layer]), lru_bx[layer], lru_lam[layer], tc, tl)
    h_lat = (h0[:, tc:] + h1[:, tc:]).reshape(b, GRID_W, rows, lw_).transpose(0, 2, 1, 3).reshape(b, tl, lw_)

    router_pad = jnp.pad(router[layer], ((0, 0), (0, LANES - n_experts)))
    x1, h2, aff = _merge(y0, y1, r, v, ks, g, h_lat, ggb, sg, x, mod2,
                         lnx_g[layer][None], lnx_b[layer][None], rwkv_r_k[layer].reshape(1, w),
                         e_heads, w_proj_rwkv[layer].astype(BF16), w_proj_lru[layer].astype(BF16),
                         w_out[layer].astype(BF16), norm2_g[layer][None], router_pad, tc, n_experts)

    slot4 = _route(aff, cap).reshape(b, n_experts, 1, tl)
    ye = _ffn(slot4, aff.reshape(b, n_experts, 1, tl), h2, moe_w1[layer].astype(BF16),
              moe_w3[layer].astype(BF16), moe_w2[layer].astype(BF16), cap)
    return _combine(slot4, ye, x1, gt2, final_g[None], cap)
```

```python
import functools

import jax
import jax.numpy as jnp
from jax import lax
from jax.experimental import pallas as pl
from jax.experimental.pallas import tpu as pltpu

F32 = jnp.float32
BF16 = jnp.bfloat16

GRID_W = 64
HEAD_DIM = 64
RWKV_W_RANK = 64
RWKV_A_RANK = 64
RWKV_G_RANK = 128
CONV_W = 4
LRU_C = 8.0
EC_CAPACITY = 2
RMS_EPS = 1e-6
LNX_EPS = 64e-5
MIN_NORMAL_BITS = 0x00800000
MIN_NORMAL = 1.1754944e-38

LANES = 128
SUBLANES = 8
CHUNK = 64
ROW_TILE = 256
VMEM_LIMIT = 56 * 1024 * 1024


def _cparams(sem):
    return pltpu.CompilerParams(dimension_semantics=sem, vmem_limit_bytes=VMEM_LIMIT)


def _mm(a, b):
    return jnp.dot(a.astype(BF16), b.astype(BF16), preferred_element_type=F32)


def _split3(a):
    hi = a.astype(BF16)
    r1 = a - hi.astype(F32)
    mid = r1.astype(BF16)
    lo = (r1 - mid.astype(F32)).astype(BF16)
    return hi, mid, lo


def _mm_split_rhs(a, b01):
    hi = a.astype(BF16)
    lo = (a - hi.astype(F32)).astype(BF16)
    d = lambda z: jnp.dot(z, b01, preferred_element_type=F32)
    return d(hi) + d(lo)


def _mm_exact_lhs(a01, b):
    hi, mid, lo = _split3(b)
    d = lambda z: jnp.dot(a01, z, preferred_element_type=F32)
    return d(hi) + d(mid) + d(lo)


def _softplus(z):
    return jnp.maximum(z, 0.0) + jnp.log(1.0 + jnp.exp(-jnp.abs(z)))


def _sigmoid(z):
    return 1.0 / (1.0 + jnp.exp(-z))


def _mod_kernel(c_ref, w_ref, b_ref, o_ref):
    c = c_ref[...]
    s = c * _sigmoid(c)
    o_ref[...] = jnp.dot(s, w_ref[...], precision=lax.Precision.HIGHEST,
                         preferred_element_type=F32) + b_ref[...]


def _modulation(c_rows, w_mod, b_mod):
    rows, d = c_rows.shape
    n = w_mod.shape[1]
    tn = 1536
    return pl.pallas_call(
        _mod_kernel,
        grid=(n // tn,),
        in_specs=[pl.BlockSpec((rows, d), lambda j: (0, 0)),
                  pl.BlockSpec((d, tn), lambda j: (0, j)),
                  pl.BlockSpec((1, tn), lambda j: (0, j))],
        out_specs=pl.BlockSpec((rows, tn), lambda j: (0, j)),
        out_shape=jax.ShapeDtypeStruct((rows, n), F32),
        compiler_params=_cparams(("arbitrary",)),
        name="mod",
    )(c_rows, w_mod, b_mod.reshape(1, n))


def _inproj_kernel(ctx_ref, x_ref, g_ref, mod_ref, w_ref, rw_ref, xb_ref, ggb_ref, sg_ref,
                   *, cols, ntc):
    x = jnp.where(pl.program_id(1) < ntc, ctx_ref[0], x_ref[0])
    xn = x * lax.rsqrt(jnp.mean(x * x, axis=-1, keepdims=True) + RMS_EPS) * g_ref[...]
    sh = mod_ref[0, 0, 0:1, :]
    sc = mod_ref[0, 0, 1:2, :]
    u = (xn * (1.0 + sc) + sh).astype(BF16)
    c_rw, c_xb, c_gb = cols
    d = lambda lo, hi: jnp.dot(u, w_ref[:, lo:hi], preferred_element_type=F32)
    rw_ref[0] = d(0, c_rw)
    xb_ref[0] = d(c_rw, c_xb)
    ggb_ref[0] = jax.nn.gelu(d(c_xb, c_gb))
    sg_ref[0] = _sigmoid(d(c_gb, w_ref.shape[1]))


def _inproj(ctx, x, g1, mod1, w_in_bf, rwkv_cols, lru_w):
    b, tc, d = ctx.shape
    t = tc + x.shape[1]
    n = w_in_bf.shape[1]
    tm = min(ROW_TILE, tc)
    ntc = tc // tm
    cols = (rwkv_cols, rwkv_cols + lru_w, rwkv_cols + 2 * lru_w)
    gate_cols = n - cols[2]
    row_spec = lambda w: pl.BlockSpec((1, tm, w), lambda bi, i: (bi, i, 0))
    return pl.pallas_call(
        functools.partial(_inproj_kernel, cols=cols, ntc=ntc),
        grid=(b, t // tm),
        in_specs=[pl.BlockSpec((1, tm, d), lambda bi, i: (bi, jnp.minimum(i, ntc - 1), 0)),
                  pl.BlockSpec((1, tm, d), lambda bi, i: (bi, jnp.maximum(i - ntc, 0), 0)),
                  pl.BlockSpec((1, d), lambda bi, i: (0, 0)),
                  pl.BlockSpec((1, 1, 2, d), lambda bi, i: (bi, jnp.where(i < ntc, 0, 1), 0, 0)),
                  pl.BlockSpec((d, n), lambda bi, i: (0, 0))],
        out_specs=[row_spec(rwkv_cols), row_spec(lru_w), row_spec(lru_w), row_spec(gate_cols)],
        out_shape=[jax.ShapeDtypeStruct((b, t, rwkv_cols), F32),
                   jax.ShapeDtypeStruct((b, t, lru_w), F32),
                   jax.ShapeDtypeStruct((b, t, lru_w), F32),
                   jax.ShapeDtypeStruct((b, t, gate_cols), F32)],
        compiler_params=_cparams(("parallel", "parallel")),
        name="inproj",
    )(ctx, x, g1, mod1, w_in_bf)


def _shift_down(z, halo_prev, s):
    tm = z.shape[0]
    rolled = pltpu.roll(z, s, 0)
    hp = pltpu.roll(halo_prev, s, 0)
    r8 = lax.broadcasted_iota(jnp.int32, (SUBLANES, 1), 0)
    head = jnp.where(r8 < s, hp, rolled[0:SUBLANES])
    return jnp.concatenate([head, rolled[SUBLANES:tm]], axis=0)


def _shift_up(z, halo_next, s):
    tm = z.shape[0]
    rolled = pltpu.roll(z, tm - s, 0)
    hn = pltpu.roll(halo_next, SUBLANES - s, 0)
    r8 = lax.broadcasted_iota(jnp.int32, (SUBLANES, 1), 0)
    tail = jnp.where(r8 >= SUBLANES - s, hn, rolled[tm - SUBLANES:tm])
    return jnp.concatenate([rolled[0:tm - SUBLANES], tail], axis=0)


def _seg_pos(i, tm, tc, tl):
    grow = lax.broadcasted_iota(jnp.int32, (tm, 1), 0) + i * tm
    in_ctx = grow < tc
    pos = jnp.where(in_ctx, grow, grow - tc)
    seg_len = jnp.where(in_ctx, tc, tl)
    return pos, seg_len


def _halo_specs(tm, t, width):
    nb = tm // SUBLANES
    last = t // SUBLANES - 1
    prev = pl.BlockSpec((1, SUBLANES, width), lambda bi, i: (bi, jnp.maximum(i * nb - 1, 0), 0))
    nxt = pl.BlockSpec((1, SUBLANES, width), lambda bi, i: (bi, jnp.minimum((i + 1) * nb, last), 0))
    return prev, nxt


def _chunk_masks(rev):
    c = CHUNK
    n = 2 * c
    sign = -1 if rev else 1
    ti = lax.broadcasted_iota(jnp.int32, (c, c), 0)
    ii = lax.broadcasted_iota(jnp.int32, (c, c), 1)
    row = lax.broadcasted_iota(jnp.int32, (c, n), 0)
    idx = lax.broadcasted_iota(jnp.int32, (c, n), 1) % c
    delta = (idx - row) * sign
    lane = lax.broadcasted_iota(jnp.int32, (1, n), 1)
    return {
        "cum": jnp.where((ii - ti) * sign <= 0, 1.0, 0.0).astype(BF16),
        "strict": delta < 0,
        "incl": delta <= 0,
        "eye": jnp.where(idx == row, 1.0, 0.0).astype(F32),
        "m0": jnp.where(lane < c, 1.0, 0.0).astype(BF16),
        "m1": jnp.where(lane >= c, 1.0, 0.0).astype(BF16),
    }


def _chunk_group(units, m, nsq):
    c = CHUNK
    n = 2 * c
    bf = lambda z: z.astype(BF16)
    dot = lambda x, y: jnp.dot(x, y, preferred_element_type=F32)
    m0, m1 = m["m0"], m["m1"]
    blockdiag = lambda zb: jnp.concatenate([zb * m0, zb * m1], axis=0)

    def head_t(z):
        zt = jnp.transpose(z)
        return jnp.concatenate([zt[0:c], zt[c:n]], axis=1)

    lcs = [_mm_exact_lhs(m["cum"], u[3]) for u in units]
    pre = []
    for (r, v, kk, lw, kd, bd), lc in zip(units, lcs):
        ltot = jnp.sum(lw, axis=0, keepdims=True)
        en = jnp.exp(-lc)
        eh = jnp.exp(ltot - lc)
        qr = r * jnp.exp(lc)
        qa = bf(-kk * jnp.exp(lc - lw))
        v_bd = blockdiag(bf(v))
        pre.append(dict(q=jnp.concatenate([qa, bf(qr)], axis=0), qa_bd=blockdiag(qa), qr=qr,
                        kbx=jnp.concatenate([blockdiag(bf(bd * en)), blockdiag(bf(kd * en))], axis=0),
                        v_bd=v_bd, v0=jnp.concatenate([jnp.zeros_like(v_bd), v_bd], axis=1),
                        kt=bf(jnp.concatenate([head_t(bd * eh), head_t(kd * eh)], axis=1)),
                        dg=m["eye"] * jnp.exp(ltot)))

    gram = [lax.dot_general(p["q"], p["kbx"], (((1,), (1,)), ((), ())), preferred_element_type=F32)
            for p in pre]
    gab_f = [jnp.where(m["strict"], g[:c, :n], 0.0) for g in gram]
    gak = [bf(jnp.where(m["strict"], g[:c, n:], 0.0)) for g in gram]
    grbk = [bf(jnp.concatenate([jnp.where(m["incl"], g[c:, :n], 0.0),
                                jnp.where(m["incl"], g[c:, n:], 0.0)], axis=1)) for g in gram]

    t_inv = [m["eye"] + g for g in gab_f]
    gp = [bf(g) for g in gab_f]
    gp = [bf(dot(g, blockdiag(g))) for g in gp]
    for _ in range(nsq - 1):
        prod = [dot(jnp.concatenate([bf(t), g], axis=0), blockdiag(g)) for t, g in zip(t_inv, gp)]
        t_inv = [t + p[:c] for t, p in zip(t_inv, prod)]
        gp = [bf(p[c:]) for p in prod]
    t_inv = [bf(t + dot(bf(t), blockdiag(g))) for t, g in zip(t_inv, gp)]

    gv = [bf(dot(g, p["v_bd"])) for g, p in zip(gak, pre)]
    tw = [bf(dot(t, jnp.concatenate([p["qa_bd"], blockdiag(x)], axis=1)))
          for t, p, x in zip(t_inv, pre, gv)]
    big = [dot(jnp.concatenate([g, p["kt"]], axis=0),
               jnp.concatenate([jnp.concatenate([blockdiag(x[:, :n]), blockdiag(x[:, n:])], axis=1),
                                p["v0"]], axis=0))
           for g, p, x in zip(grbk, pre, tw)]
    return [(p["qr"] + o[:c, :n], o[:c, n:], p["dg"] + o[c:, :n], o[c:, n:])
            for p, o in zip(pre, big)]


def _front_kernel(rw_ref, prev_ref, next_ref, mu_ref, kkw_ref, ka_ref, w0_ref, a0_ref,
                  w2_ref, a2_ref, g2_ref, e_ref,
                  r_ref, v_ref, g_ref, ks_ref, ra_ref, yv_ref, ac_ref, sc_ref,
                  *, tc, tl, width, nsq):
    i = pl.program_id(1)
    p = rw_ref[0]
    tm = p.shape[0]
    pos, seg_len = _seg_pos(i, tm, tc, tl)
    prev = jnp.where(pos >= 1, _shift_down(p, prev_ref[0], 1), 0.0)
    nxt = jnp.where(pos + 1 < seg_len, _shift_up(p, next_ref[0], 1), 0.0)
    pm = p + mu_ref[...] * (0.5 * (prev + nxt) - p)

    w = width
    r = pm[:, 0:w]
    k = pm[:, w:2 * w]
    v = pm[:, 2 * w:3 * w]
    wa = pm[:, 3 * w:3 * w + RWKV_W_RANK + RWKV_A_RANK]
    gd = pm[:, 3 * w + RWKV_W_RANK + RWKV_A_RANK:]

    kkr = k * kkw_ref[...]
    ss = _mm_split_rhs(kkr * kkr, e_ref[...])
    kk = kkr / jnp.maximum(jnp.sqrt(ss), 1e-12)
    r_ref[0] = r
    v_ref[0] = v
    g_ref[0] = _mm(_sigmoid(gd), g2_ref[...])

    twa = jnp.tanh(wa)
    ka = ka_ref[...]
    per_dir = []
    ksum = jnp.zeros_like(k)
    for d in range(2):
        wpre = w0_ref[d:d + 1, :] + _mm(twa, w2_ref[d])
        lw = -jnp.exp(-_softplus(-wpre) - 0.5)
        eta = _sigmoid(a0_ref[d:d + 1, :] + _mm(wa, a2_ref[d]))
        kd = k * (1.0 + (eta - 1.0) * ka)
        per_dir.append((lw, kd, kk * eta))
        ksum = ksum + kd
    ks_ref[0] = ksum

    n = 2 * CHUNK
    for d, (lw, kd, bd) in enumerate(per_dir):
        m = _chunk_masks(d == 1)
        keys, units = [], []
        for ci in range(tm // CHUNK):
            rows = slice(ci * CHUNK, (ci + 1) * CHUNK)
            for hp in range(w // n):
                lanes = slice(hp * n, (hp + 1) * n)
                keys.append((ci, hp, rows, lanes))
                units.append((r[rows, lanes], v[rows, lanes], kk[rows, lanes],
                              lw[rows, lanes], kd[rows, lanes], bd[rows, lanes]))
        for (ci, hp, rows, lanes), (ra, yv, ac, sc) in zip(keys, _chunk_group(units, m, nsq)):
            ra_ref[d, 0, rows, lanes] = ra
            yv_ref[d, 0, rows, lanes] = yv
            ac_ref[d, 0, ci, hp] = ac
            sc_ref[d, 0, ci, hp] = sc


def _rwkv_front(rw, mu, k_k, k_a, w0, a0, w2p, a2p, g2, e_heads, tc, tl):
    b, t, cols = rw.shape
    w = k_k.shape[-1]
    tm = min(ROW_TILE, tc)
    n = 2 * CHUNK
    nc = t // CHUNK
    cpt = tm // CHUNK
    nsq = (CHUNK - 1).bit_length() - 1
    prev_spec, next_spec = _halo_specs(tm, t, cols)
    full = lambda shape: pl.BlockSpec(shape, lambda bi, i: (0,) * len(shape))
    row = pl.BlockSpec((1, tm, w), lambda bi, i: (bi, i, 0))
    drow = pl.BlockSpec((2, 1, tm, w), lambda bi, i: (0, bi, i, 0))
    mat = pl.BlockSpec((2, 1, cpt, w // n, CHUNK, n), lambda bi, i: (0, bi, i, 0, 0, 0))
    shp = jax.ShapeDtypeStruct((b, t, w), F32)
    dshp = jax.ShapeDtypeStruct((2, b, t, w), F32)
    mshp = jax.ShapeDtypeStruct((2, b, nc, w // n, CHUNK, n), F32)
    return pl.pallas_call(
        functools.partial(_front_kernel, tc=tc, tl=tl, width=w, nsq=nsq),
        grid=(b, t // tm),
        in_specs=[pl.BlockSpec((1, tm, cols), lambda bi, i: (bi, i, 0)), prev_spec, next_spec,
                  full((1, cols)), full((1, w)), full((1, w)), full((2, w)), full((2, w)),
                  full(w2p.shape), full(a2p.shape), full(g2.shape), full(e_heads.shape)],
        out_specs=[row, row, row, row, drow, drow, mat, mat],
        out_shape=[shp, shp, shp, shp, dshp, dshp, mshp, mshp],
        compiler_params=_cparams(("parallel", "parallel")),
        name="front",
    )(rw, rw, rw, mu, k_k, k_a, w0, a0, w2p, a2p, g2, e_heads)


def _carry_kernel(ra0, yv0, ac0, sc0, ra1, yv1, ac1, sc1, y0_ref, y1_ref, st_ref):
    j = pl.program_id(0)

    @pl.when(j == 0)
    def _():
        st_ref[...] = jnp.zeros_like(st_ref)

    nb, c, w = y0_ref.shape
    n = 2 * c
    lane = lax.broadcasted_iota(jnp.int32, (1, n), 1)
    m0 = jnp.where(lane < c, 1.0, 0.0).astype(BF16)
    m1 = jnp.where(lane >= c, 1.0, 0.0).astype(BF16)
    blockdiag = lambda zb: jnp.concatenate([zb * m0, zb * m1], axis=0)
    dot = lambda x, y: jnp.dot(x, y, preferred_element_type=F32)

    srcs = ((ra0, yv0, ac0, sc0, y0_ref), (ra1, yv1, ac1, sc1, y1_ref))
    units = [(d, bi, hp) for d in range(2) for bi in range(nb) for hp in range(w // n)]
    lanes = lambda hp: slice(hp * n, (hp + 1) * n)
    sts = [st_ref[d, bi, hp] for d, bi, hp in units]
    st_hi = [s.astype(BF16) for s in sts]
    st_lo = [(s - h.astype(F32)).astype(BF16) for s, h in zip(sts, st_hi)]
    bd_hi = [blockdiag(h) for h in st_hi]
    bd_lo = [blockdiag(l) for l in st_lo]
    acs = [srcs[d][2][0, bi, 0, hp] for d, bi, hp in units]
    ac_hi = [a.astype(BF16) for a in acs]
    ac_lo = [(a - h.astype(F32)).astype(BF16) for a, h in zip(acs, ac_hi)]
    prods = [dot(jnp.concatenate([srcs[d][0][0, bi, :, lanes(hp)].astype(BF16), ah, al], axis=0),
                 jnp.concatenate([bh, bl], axis=1))
             for (d, bi, hp), ah, al, bh, bl in zip(units, ac_hi, ac_lo, bd_hi, bd_lo)]
    for (d, bi, hp), p in zip(units, prods):
        srcs[d][4][bi, :, lanes(hp)] = p[:c, :n] + p[:c, n:] + srcs[d][1][0, bi, :, lanes(hp)]
        st_ref[d, bi, hp] = (p[c:2 * c, :n] + p[c:2 * c, n:] + p[2 * c:, :n]
                             + srcs[d][3][0, bi, 0, hp])


def _rwkv_carry(ra, yv, ac, sc, tc):
    _, b, t, w = ra.shape
    n = 2 * CHUNK
    nc = t // CHUNK
    ncc = tc // CHUNK
    fwd = lambda j: j
    bwd = lambda j: jnp.where(j < ncc, ncc - 1 - j, nc + ncc - 1 - j)
    row = lambda d, f: pl.BlockSpec((1, b, CHUNK, w), lambda j: (d, 0, f(j), 0))
    mat = lambda d, f: pl.BlockSpec((1, b, 1, w // n, CHUNK, n), lambda j: (d, 0, f(j), 0, 0, 0))
    out = lambda f: pl.BlockSpec((b, CHUNK, w), lambda j: (0, f(j), 0))
    shp = jax.ShapeDtypeStruct((b, t, w), F32)
    return pl.pallas_call(
        _carry_kernel,
        grid=(nc,),
        in_specs=[row(0, fwd), row(0, fwd), mat(0, fwd), mat(0, fwd),
                  row(1, bwd), row(1, bwd), mat(1, bwd), mat(1, bwd)],
        out_specs=[out(fwd), out(bwd)],
        out_shape=[shp, shp],
        scratch_shapes=[pltpu.VMEM((2, b, w // n, CHUNK, n), F32)],
        compiler_params=_cparams(("arbitrary",)),
        name="carry",
    )(ra, yv, ac, sc, ra, yv, ac, sc)


def _lru_dir(z, halo, cw_ref, cb_ref, wa_ref, ba_ref, wx_ref, bx_ref, lam_ref, pos, seg_len,
             carry, h_ref, a_scr, b_scr, d):
    rev = d == 1
    tm, w = z.shape
    xc = cb_ref[d:d + 1, :] + cw_ref[d, CONV_W - 1:CONV_W, :] * z
    for s in range(1, CONV_W):
        if rev:
            zs = jnp.where(pos + s < seg_len, _shift_up(z, halo, s), 0.0)
        else:
            zs = jnp.where(pos >= s, _shift_down(z, halo, s), 0.0)
        xc = xc + cw_ref[d, CONV_W - 1 - s:CONV_W - s, :] * zs
    rg = _sigmoid(_mm(xc, wa_ref[d]) + ba_ref[d:d + 1, :])
    ig = _sigmoid(_mm(xc, wx_ref[d]) + bx_ref[d:d + 1, :])
    log_a = -LRU_C * rg * _softplus(-lam_ref[d:d + 1, :])
    a = jnp.exp(log_a)
    q = -jnp.tanh(log_a) * (a * a + 1.0)
    bv = q * lax.rsqrt(jnp.maximum(q, MIN_NORMAL)) * (ig * xc)

    groups = tm // SUBLANES
    nl = w // LANES
    for kl in range(nl):
        a_scr[kl] = a[:, kl * LANES:(kl + 1) * LANES]
        b_scr[kl] = bv[:, kl * LANES:(kl + 1) * LANES]
    rows_of = lambda ref, r: jnp.concatenate(
        [ref[kl, pl.ds(r, groups, stride=SUBLANES), :] for kl in range(nl)], axis=1)
    order = list(reversed(range(SUBLANES))) if rev else list(range(SUBLANES))
    acum, bcum = {}, {}
    pa = pb = None
    for r in order:
        ar, br = rows_of(a_scr, r), rows_of(b_scr, r)
        pa, pb = (ar, br) if pa is None else (ar * pa, br + ar * pb)
        acum[r], bcum[r] = pa, pb
    cin = [None] * groups
    for gi in (reversed(range(groups)) if rev else range(groups)):
        cin[gi] = carry
        carry = pa[gi:gi + 1] * carry + pb[gi:gi + 1]
    cin = jnp.concatenate(cin, axis=0)
    for r in order:
        hr = acum[r] * cin + bcum[r]
        for kl in range(nl):
            a_scr[kl, pl.ds(r, groups, stride=SUBLANES), :] = hr[:, kl * LANES:(kl + 1) * LANES]
    for kl in range(nl):
        h_ref[0, :, kl * LANES:(kl + 1) * LANES] = a_scr[kl]
    return carry


def _lru_kernel(zf_ref, prev_ref, zb_ref, next_ref, cw_ref, cb_ref, wa_ref, ba_ref, wx_ref, bx_ref,
                lam_ref, h0_ref, h1_ref, carry_ref, a_scr, b_scr, *, tc, tl):
    j = pl.program_id(1)

    @pl.when(j == 0)
    def _():
        carry_ref[...] = jnp.zeros_like(carry_ref)

    tm = zf_ref.shape[1]
    nt = (tc + tl) // tm
    ntc = tc // tm
    jb = jnp.where(j < ntc, ntc - 1 - j, nt + ntc - 1 - j)
    params = (cw_ref, cb_ref, wa_ref, ba_ref, wx_ref, bx_ref, lam_ref)
    pos, seg_len = _seg_pos(j, tm, tc, tl)
    carry_ref[0] = _lru_dir(zf_ref[0], prev_ref[0], *params, pos, seg_len, carry_ref[0],
                            h0_ref, a_scr, b_scr, 0)
    pos, seg_len = _seg_pos(jb, tm, tc, tl)
    carry_ref[1] = _lru_dir(zb_ref[0], next_ref[0], *params, pos, seg_len, carry_ref[1],
                            h1_ref, a_scr, b_scr, 1)


def _lru(zs, conv_w, conv_b, wa_bd, ba, wx_bd, bx, lam, tc, tl):
    b, t, w = zs.shape
    tm = min(ROW_TILE, tc)
    nt = t // tm
    ntc = tc // tm
    nb = tm // SUBLANES
    last = t // SUBLANES - 1
    bwd = lambda j: jnp.where(j < ntc, ntc - 1 - j, nt + ntc - 1 - j)
    full = lambda shape: pl.BlockSpec(shape, lambda bi, j: (0,) * len(shape))
    tile_f = pl.BlockSpec((1, tm, w), lambda bi, j: (bi, j, 0))
    tile_b = pl.BlockSpec((1, tm, w), lambda bi, j: (bi, bwd(j), 0))
    prev = pl.BlockSpec((1, SUBLANES, w), lambda bi, j: (bi, jnp.maximum(j * nb - 1, 0), 0))
    nxt = pl.BlockSpec((1, SUBLANES, w), lambda bi, j: (bi, jnp.minimum((bwd(j) + 1) * nb, last), 0))
    shp = jax.ShapeDtypeStruct((b, t, w), F32)
    return pl.pallas_call(
        functools.partial(_lru_kernel, tc=tc, tl=tl),
        grid=(b, nt),
        in_specs=[tile_f, prev, tile_b, nxt,
                  full(conv_w.shape), full(conv_b.shape), full(wa_bd.shape), full(ba.shape),
                  full(wx_bd.shape), full(bx.shape), full(lam.shape)],
        out_specs=[tile_f, tile_b],
        out_shape=[shp, shp],
        scratch_shapes=[pltpu.VMEM((2, 1, w), F32), pltpu.VMEM((w // LANES, tm, LANES), F32),
                        pltpu.VMEM((w // LANES, tm, LANES), F32)],
        compiler_params=_cparams(("parallel", "arbitrary")),
        name="lru",
    )(zs, zs, zs, zs, conv_w, conv_b, wa_bd, ba, wx_bd, bx, lam)


def _merge_kernel(y0_ref, y1_ref, r_ref, v_ref, ks_ref, g_ref, h_ref, ggb_ref,
                  sga_ref, sgb_ref, x_ref, mod_ref, lng_ref, lnb_ref, rk_ref, e_ref,
                  wr_ref, wl_ref, wo_ref, g2_ref, rt_ref,
                  x1_ref, h2_ref, aff_ref, *, n_experts):
    e = e_ref[...]
    inv = 1.0 / HEAD_DIM
    y = y0_ref[0] + y1_ref[0]
    mean = _mm_split_rhs(y, e) * inv
    yc = y - mean
    var = _mm_split_rhs(yc * yc, e) * inv
    yn = yc * lax.rsqrt(var + LNX_EPS) * lng_ref[...] + lnb_ref[...]
    bonus = _mm_split_rhs(r_ref[0] * ks_ref[0] * rk_ref[...], e) * v_ref[0]
    ya = _mm((yn + bonus) * g_ref[0], wr_ref[...])
    yb = _mm(h_ref[0] * ggb_ref[0], wl_ref[...])
    mix = _mm(sga_ref[0] * ya + sgb_ref[0] * yb, wo_ref[...])
    gt1 = mod_ref[0, 0:1, :]
    sh2 = mod_ref[0, 1:2, :]
    sc2 = mod_ref[0, 2:3, :]
    x1 = x_ref[0] + gt1 * mix
    x1_ref[0] = x1
    xn = x1 * lax.rsqrt(jnp.mean(x1 * x1, axis=-1, keepdims=True) + RMS_EPS) * g2_ref[...]
    h2 = xn * (1.0 + sc2) + sh2
    h2_ref[0] = h2.astype(BF16)
    logits = _mm(h2, rt_ref[...])
    col = lax.broadcasted_iota(jnp.int32, logits.shape, 1)
    logits = jnp.where(col < n_experts, logits, -1e30)
    ex = jnp.exp(logits - jnp.max(logits, axis=-1, keepdims=True))
    aff = ex / jnp.sum(ex, axis=-1, keepdims=True)
    aff_ref[0] = jnp.transpose(aff)[0:n_experts, :]


def _merge(y0, y1, r, v, ks, g, h, ggb, sg, x, mod2, lnx_g, lnx_b, r_k, e_heads,
           wr, wl, wo, g2n, router_pad, tc, n_experts):
    b, tl, d = x.shape
    w = r.shape[-1]
    tm = min(ROW_TILE, tc)
    off = tc // tm
    seq = lambda width, blk=0: pl.BlockSpec((1, tm, width), lambda bi, i: (bi, i + off, blk))
    lat = lambda width: pl.BlockSpec((1, tm, width), lambda bi, i: (bi, i, 0))
    full = lambda shape: pl.BlockSpec(shape, lambda bi, i: (0,) * len(shape))
    return pl.pallas_call(
        functools.partial(_merge_kernel, n_experts=n_experts),
        grid=(b, tl // tm),
        in_specs=[seq(w), seq(w), seq(w), seq(w), seq(w), seq(w), lat(w), seq(w),
                  seq(d, 0), seq(d, 1), lat(d),
                  pl.BlockSpec((1, 3, d), lambda bi, i: (bi, 0, 0)),
                  full((1, w)), full((1, w)), full((1, w)), full(e_heads.shape),
                  full(wr.shape), full(wl.shape), full(wo.shape), full((1, d)),
                  full(router_pad.shape)],
        out_specs=[lat(d), lat(d),
                   pl.BlockSpec((1, n_experts, tm), lambda bi, i: (bi, 0, i))],
        out_shape=[jax.ShapeDtypeStruct((b, tl, d), F32),
                   jax.ShapeDtypeStruct((b, tl, d), BF16),
                   jax.ShapeDtypeStruct((b, n_experts, tl), F32)],
        compiler_params=_cparams(("parallel", "parallel")),
        name="merge",
    )(y0, y1, r, v, ks, g, h, ggb, sg, sg, x, mod2, lnx_g, lnx_b, r_k, e_heads,
      wr, wl, wo, g2n, router_pad)


def _prefix_incl(x, lane):
    n = x.shape[-1]
    s = 1
    while s < n:
        x = x + jnp.where(lane >= s, pltpu.roll(x, s, 1), 0.0)
        s *= 2
    return x


def _route_kernel(aff_ref, slot_ref, *, cap):
    aff = aff_ref[0]
    ne, t = aff.shape
    count_ge = lambda v: jnp.sum(jnp.where(aff >= v, 1.0, 0.0), axis=-1, keepdims=True)

    def bit_step(i, thr):
        cand = thr | jnp.left_shift(jnp.int32(1), 30 - i)
        return jnp.where(count_ge(pltpu.bitcast(cand, F32)) >= cap, cand, thr)

    thr = lax.fori_loop(0, 31, bit_step, jnp.zeros((ne, 1), jnp.int32))
    lo = pltpu.bitcast(thr, F32)
    hi = pltpu.bitcast(jnp.maximum(thr + 1, MIN_NORMAL_BITS), F32)

    def halve(_, lh):
        lo, hi = lh
        mid = lo + 0.5 * (hi - lo)
        ge = count_ge(mid) >= cap
        return jnp.where(ge, mid, lo), jnp.where(ge, hi, mid)

    lo, hi = lax.fori_loop(0, 30, halve, (lo, hi))
    gt = aff >= hi
    eq = jnp.where((aff >= lo) & (aff < hi), 1.0, 0.0)
    need = cap - jnp.sum(jnp.where(gt, 1.0, 0.0), axis=-1, keepdims=True)
    lane = lax.broadcasted_iota(jnp.int32, (ne, t), 1)
    eq_rank = _prefix_incl(eq, lane) - eq
    sel = jnp.where(gt, 1.0, jnp.where(eq_rank < need, eq, 0.0))
    slot = _prefix_incl(sel, lane) - sel
    slot_ref[0] = jnp.where(sel > 0.0, slot, -1.0)


def _route(aff, cap):
    b, ne, t = aff.shape
    spec = pl.BlockSpec((1, ne, t), lambda bi: (bi, 0, 0))
    return pl.pallas_call(
        functools.partial(_route_kernel, cap=cap),
        grid=(b,),
        in_specs=[spec],
        out_specs=spec,
        out_shape=jax.ShapeDtypeStruct((b, ne, t), F32),
        compiler_params=_cparams(("parallel",)),
        name="route",
    )(aff)


def _ffn_kernel(slot_ref, aff_ref, h2_ref, w1_ref, w3_ref, w2_ref, y_ref, *, cap):
    slots = lax.broadcasted_iota(jnp.int32, (cap, 1), 0).astype(F32)
    hit = slot_ref[0, 0] == slots
    gate = jnp.sum(jnp.where(hit, aff_ref[0, 0], 0.0), axis=-1, keepdims=True)
    onehot = jnp.where(hit, 1.0, 0.0).astype(BF16)
    xs = jnp.dot(onehot, h2_ref[0], preferred_element_type=F32).astype(BF16)
    a = jnp.dot(xs, w1_ref[0], preferred_element_type=F32)
    hid = (a * _sigmoid(a)) * jnp.dot(xs, w3_ref[0], preferred_element_type=F32)
    y = jnp.dot(hid.astype(BF16), w2_ref[0], preferred_element_type=F32) * gate
    y_ref[0, 0] = y.astype(BF16)


def _ffn(slot4, aff4, h2, w1, w3, w2, cap):
    b, ne, _, t = slot4.shape
    d = h2.shape[-1]
    f = w1.shape[-1]
    row = pl.BlockSpec((1, 1, 1, t), lambda bi, e: (bi, e, 0, 0))
    return pl.pallas_call(
        functools.partial(_ffn_kernel, cap=cap),
        grid=(b, ne),
        in_specs=[row, row,
                  pl.BlockSpec((1, t, d), lambda bi, e: (bi, 0, 0)),
                  pl.BlockSpec((1, d, f), lambda bi, e: (e, 0, 0)),
                  pl.BlockSpec((1, d, f), lambda bi, e: (e, 0, 0)),
                  pl.BlockSpec((1, f, d), lambda bi, e: (e, 0, 0))],
        out_specs=pl.BlockSpec((1, 1, cap, d), lambda bi, e: (bi, e, 0, 0)),
        out_shape=jax.ShapeDtypeStruct((b, ne, cap, d), BF16),
        compiler_params=_cparams(("parallel", "arbitrary")),
        name="ffn",
    )(slot4, aff4, h2, w1, w3, w2)


def _combine_kernel(slot_ref, y_ref, x1_ref, gt_ref, g_ref, o_ref, acc_ref, *, cap):
    e = pl.program_id(2)

    @pl.when(e == 0)
    def _():
        acc_ref[...] = jnp.zeros_like(acc_ref)

    slots = lax.broadcasted_iota(jnp.int32, (cap, 1), 0).astype(F32)
    onehot = jnp.where(slot_ref[0, 0] == slots, 1.0, 0.0).astype(BF16)
    acc_ref[...] += lax.dot_general(onehot, y_ref[0, 0], (((0,), (0,)), ((), ())),
                                    preferred_element_type=F32)

    @pl.when(e == pl.num_programs(2) - 1)
    def _():
        x2 = x1_ref[0] + gt_ref[0] * acc_ref[...]
        o_ref[0] = x2 * lax.rsqrt(jnp.mean(x2 * x2, axis=-1, keepdims=True) + RMS_EPS) * g_ref[...]


def _combine(slot4, ye, x1, gt2, final_g, cap):
    b, tl, d = x1.shape
    ne = slot4.shape[1]
    tt = min(8 * ROW_TILE, tl)
    row = pl.BlockSpec((1, tt, d), lambda bi, j, e: (bi, j, 0))
    return pl.pallas_call(
        functools.partial(_combine_kernel, cap=cap),
        grid=(b, tl // tt, ne),
        in_specs=[pl.BlockSpec((1, 1, 1, tt), lambda bi, j, e: (bi, e, 0, j)),
                  pl.BlockSpec((1, 1, cap, d), lambda bi, j, e: (bi, e, 0, 0)),
                  row,
                  pl.BlockSpec((1, 1, d), lambda bi, j, e: (bi, 0, 0)),
                  pl.BlockSpec((1, d), lambda bi, j, e: (0, 0))],
        out_specs=row,
        out_shape=jax.ShapeDtypeStruct((b, tl, d), F32),
        scratch_shapes=[pltpu.VMEM((tt, d), F32)],
        compiler_params=_cparams(("parallel", "parallel", "arbitrary")),
        name="combine",
    )(slot4, ye, x1, gt2, final_g)


def _block_diag(w):
    nb, n, _ = w.shape
    eye = jnp.eye(nb, dtype=w.dtype)
    return jnp.einsum("hij,hg->higj", w, eye).reshape(nb * n, nb * n)


def kernel(x, c, ctx, c_ctx, w_mod, b_mod, norm1_g, norm2_g, w_in, mu_rwkv, rwkv_w0, rwkv_w2, rwkv_a0, rwkv_a2, rwkv_g2, rwkv_k_k, rwkv_k_a, rwkv_r_k, lnx_g, lnx_b, w_proj_rwkv, conv_w, conv_b, lru_wa, lru_ba, lru_wx, lru_bx, lru_lam, w_proj_lru, w_out, router, moe_w1, moe_w3, moe_w2, final_g):
    depth = w_mod.shape[0]
    assert depth == 1, "single-layer configuration"
    b, tl, d = x.shape
    tc = ctx.shape[1]
    w = rwkv_k_k.shape[-1]
    lw_ = lru_lam.shape[-1]
    rwkv_cols = mu_rwkv.shape[-1]
    n_experts = router.shape[-1]
    cap = EC_CAPACITY * tl // n_experts
    tm = min(ROW_TILE, tc)
    assert tc % tm == 0 and tl % tm == 0 and tm % CHUNK == 0
    layer = 0

    pad_rows = -(b + 1) % SUBLANES
    c_rows = jnp.concatenate([c, c_ctx[None], jnp.zeros((pad_rows, d), F32)], axis=0)
    m = _modulation(c_rows, w_mod[layer], b_mod[layer])
    m_lat = m[:b].reshape(b, 6, d)
    m_ctx = jnp.broadcast_to(m[b].reshape(1, 6, d), (b, 6, d))
    mod1 = jnp.stack([m_ctx[:, 0:2], m_lat[:, 0:2]], axis=1)
    mod2 = m_lat[:, 2:5]
    gt2 = m_lat[:, 5:6]

    rw, xb, ggb, sg = _inproj(ctx, x, norm1_g[layer][None], mod1, w_in[layer].astype(BF16),
                              rwkv_cols, lw_)

    heads = w // HEAD_DIM
    e_heads = _block_diag(jnp.ones((heads, HEAD_DIM, HEAD_DIM), BF16))
    zpad = jnp.zeros((2, RWKV_W_RANK, w), F32)
    w2p = jnp.concatenate([rwkv_w2[layer], zpad], axis=1)
    a2p = jnp.concatenate([zpad, rwkv_a2[layer]], axis=1)
    r, v, g, ks, ra, yv, ac, sc = _rwkv_front(
        rw, mu_rwkv[layer][None], rwkv_k_k[layer][None], rwkv_k_a[layer][None],
        rwkv_w0[layer], rwkv_a0[layer], w2p, a2p, rwkv_g2[layer], e_heads, tc, tl)
    y0, y1 = _rwkv_carry(ra, yv, ac, sc, tc)

    rows = tl // GRID_W
    xb_lat = xb[:, tc:].reshape(b, rows, GRID_W, lw_).transpose(0, 2, 1, 3).reshape(b, tl, lw_)
    zs = jnp.concatenate([xb[:, :tc], xb_lat], axis=1)
    h0, h1 = _lru(zs, conv_w[layer], conv_b[layer],
                  jax.vmap(_block_diag)(lru_wa[layer]), lru_ba[layer],
                  jax.vmap(_block_diag)(lru_wx[layer]), lru_bx[layer], lru_lam[layer], tc, tl)
    h_lat = (h0[:, tc:] + h1[:, tc:]).reshape(b, GRID_W, rows, lw_).transpose(0, 2, 1, 3).reshape(b, tl, lw_)

    router_pad = jnp.pad(router[layer], ((0, 0), (0, LANES - n_experts)))
    x1, h2, aff = _merge(y0, y1, r, v, ks, g, h_lat, ggb, sg, x, mod2,
                         lnx_g[layer][None], lnx_b[layer][None], rwkv_r_k[layer].reshape(1, w),
                         e_heads, w_proj_rwkv[layer].astype(BF16), w_proj_lru[layer].astype(BF16),
                         w_out[layer].astype(BF16), norm2_g[layer][None], router_pad, tc, n_experts)

    slot4 = _route(aff, cap).reshape(b, n_experts, 1, tl)
    ye = _ffn(slot4, aff.reshape(b, n_experts, 1, tl), h2, moe_w1[layer].astype(BF16),
              moe_w3[layer].astype(BF16), moe_w2[layer].astype(BF16), cap)
    return _combine(slot4, ye, x1, gt2, final_g[None], cap)
```
